```python
import math
import jax, jax.numpy as jnp
from jax import lax
import numpy as np

D_MODEL = 1024
BATCH = 16
SEQ = 256
DEPTH = 2
DEC_BATCH = 2
DEC_SEQ = 4096
PAST_LEN = 512

GRID_W = 64
N_HEADS = 4
HEAD_DIM = 128
MIX_W = N_HEADS * HEAD_DIM
N_DIR = 2
CHUNK = 64
CONV_K = 5
ROPE_BASE = 10000.0
N_KEYS = 128
N_EXPERTS = N_KEYS * N_KEYS
R_HEADS = 8
KEY_DIM = 128
HALF_KEY = KEY_DIM // 2
PEER_TOPK = 16
PEER_BLOCK = 128
ALPHA = (2 * DEPTH) ** 0.25
INIT_BETA = (8 * DEPTH) ** -0.25
LN_EPS = 1e-5
NORM_EPS = 1e-6
SPLIT_SIZES = (3 * MIX_W, MIX_W, N_DIR * N_HEADS, N_DIR * N_HEADS, 3 * MIX_W, MIX_W,
               MIX_W, N_DIR * MIX_W, MIX_W, MIX_W, 3 * D_MODEL)
SPLIT_POINTS = tuple(int(s) for s in np.cumsum(SPLIT_SIZES)[:-1])
IN_COLS = int(sum(SPLIT_SIZES))

kernel_name = "hybrid_diffusion_trunk_step"


def _layernorm(x, g, b):
    xf = x.astype(jnp.float32)
    mu = jnp.mean(xf, -1, keepdims=True)
    var = jnp.mean(jnp.square(xf - mu), -1, keepdims=True)
    return ((xf - mu) * lax.rsqrt(var + LN_EPS) * g.astype(jnp.float32) + b.astype(jnp.float32)).astype(x.dtype)


def _rmsnorm(x, w):
    return x * lax.rsqrt(jnp.mean(jnp.square(x), -1, keepdims=True) + NORM_EPS) * w.astype(jnp.float32)


def _headnorm(x, w):
    mu = jnp.mean(x, -1, keepdims=True)
    var = jnp.mean(jnp.square(x - mu), -1, keepdims=True)
    return (x - mu) * lax.rsqrt(var + NORM_EPS) * w.astype(jnp.float32)


def _l2norm(x):
    return x * lax.rsqrt(jnp.sum(jnp.square(x), -1, keepdims=True) + NORM_EPS)


def _to_chunks(a):
    b, t = a.shape[:2]
    a = a.reshape(b, t // CHUNK, CHUNK, *a.shape[2:])
    return jnp.swapaxes(jnp.swapaxes(a, 0, 1), 2, 3)


def _from_chunks(o):
    o = jnp.swapaxes(jnp.swapaxes(o, 2, 3), 0, 1)
    return o.reshape(o.shape[0], o.shape[1] * o.shape[2], *o.shape[3:])


def _gated_delta_chunked(q, k, v, g, beta, s0):
    qc, kc, vc = _to_chunks(q), _to_chunks(k), _to_chunks(v)
    gc = jnp.cumsum(_to_chunks(g), axis=-1)
    bc = _to_chunks(beta)[..., None]
    tri = jnp.tril(jnp.ones((CHUNK, CHUNK), bool))
    strict = jnp.tril(jnp.ones((CHUNK, CHUNK), bool), -1)
    dmask = jnp.exp(jnp.where(tri, gc[..., :, None] - gc[..., None, :], -jnp.inf))
    kb = kc * bc
    lower = jnp.where(strict, jnp.einsum('nbhik,nbhjk->nbhij', kb, kc) * dmask, 0.0)
    a_mat = lower + jnp.eye(CHUNK, dtype=lower.dtype)
    u = lax.linalg.triangular_solve(a_mat, vc * bc, left_side=True, lower=True, unit_diagonal=True)
    w = lax.linalg.triangular_solve(a_mat, kb * jnp.exp(gc)[..., None], left_side=True, lower=True,
                                    unit_diagonal=True)

    def step(s, inp):
        qi, ki, ui, wi, gi, di = inp
        v_new = ui - jnp.einsum('bhck,bhkv->bhcv', wi, s)
        attn = jnp.einsum('bhik,bhjk->bhij', qi, ki) * di
        o = (jnp.einsum('bhck,bhkv->bhcv', qi * jnp.exp(gi)[..., None], s)
             + jnp.einsum('bhij,bhjv->bhiv', attn, v_new))
        g_last = gi[..., -1:]
        s = (s * jnp.exp(g_last)[..., None]
             + jnp.einsum('bhck,bhcv->bhkv', ki * jnp.exp(g_last - gi)[..., None], v_new))
        return s, o

    s_fin, o = lax.scan(step, s0, (qc, kc, u, w, gc, dmask))
    return _from_chunks(o), s_fin


def _retention_chunked(q, k, v, log_gamma, s0):
    qc, kc, vc = _to_chunks(q), _to_chunks(k), _to_chunks(v)
    pos = jnp.arange(CHUNK, dtype=jnp.float32)
    lg = log_gamma[:, None]
    rel = pos[:, None] - pos[None, :]
    decay = jnp.exp(jnp.where(rel >= 0, rel * lg[..., None], -jnp.inf))
    q_decay = jnp.exp((pos + 1.0) * lg)[None, :, :, None]
    k_decay = jnp.exp((CHUNK - 1.0 - pos) * lg)[None, :, :, None]
    c_decay = jnp.exp(CHUNK * log_gamma)[None, :, None, None]

    def step(s, inp):
        qi, ki, vi = inp
        attn = jnp.einsum('bhik,bhjk->bhij', qi, ki) * decay
        o = (jnp.einsum('bhij,bhjv->bhiv', attn, vi)
             + jnp.einsum('bhck,bhkv->bhcv', qi * q_decay, s))
        s = s * c_decay + jnp.einsum('bhck,bhcv->bhkv', ki * k_decay, vi)
        return s, o

    s_fin, o = lax.scan(step, s0, (qc, kc, vc))
    return _from_chunks(o), s_fin


def _gla_chunked(q, k, v, log_f, s0):
    qc, kc, vc = _to_chunks(q), _to_chunks(k), _to_chunks(v)
    gc = jnp.cumsum(_to_chunks(log_f), axis=-2)
    tri = jnp.tril(jnp.ones((CHUNK, CHUNK), bool))[..., None]

    def step(s, inp):
        qi, ki, vi, gi = inp
        rel = jnp.exp(jnp.where(tri, gi[:, :, :, None, :] - gi[:, :, None, :, :], -jnp.inf))
        attn = jnp.einsum('bhik,bhjk,bhijk->bhij', qi, ki, rel)
        o = (jnp.einsum('bhij,bhjv->bhiv', attn, vi)
             + jnp.einsum('bhck,bhkv->bhcv', qi * jnp.exp(gi), s))
        g_last = gi[:, :, -1:, :]
        s = (s * jnp.exp(g_last[:, :, 0, :, None])
             + jnp.einsum('bhck,bhcv->bhkv', ki * jnp.exp(g_last - gi), vi))
        return s, o

    s_fin, o = lax.scan(step, s0, (qc, kc, vc, gc))
    return _from_chunks(o), s_fin


def _two_way(core, seq_args, par_args, s0):
    outs, finals = [], []
    for d in range(N_DIR):
        seq = seq_args[d]
        if d == 1:
            seq = tuple(jnp.flip(a, 1) for a in seq)
        o, s = core(*seq, *par_args[d], s0[:, d].astype(jnp.float32))
        outs.append(jnp.flip(o, 1) if d == 1 else o)
        finals.append(s)
    return outs[0] + outs[1], jnp.stack(finals, axis=1)


def _short_conv(x, w):
    return lax.conv_general_dilated(x, w[:, None, :].astype(x.dtype), window_strides=(1,),
                                    padding=[(CONV_K // 2, CONV_K // 2)],
                                    dimension_numbers=('NWC', 'WIO', 'NWC'),
                                    feature_group_count=x.shape[-1])


def _grid_rotary(n_tok):
    rows = n_tok // GRID_W
    r = jnp.repeat(jnp.arange(rows), GRID_W).astype(jnp.float32)
    col = jnp.tile(jnp.arange(GRID_W), rows).astype(jnp.float32)
    n_freq = HEAD_DIM // 4
    inv = ROPE_BASE ** (-jnp.arange(n_freq, dtype=jnp.float32) / n_freq)
    ang = jnp.concatenate([r[:, None] * inv, col[:, None] * inv], -1)
    return jnp.cos(ang), jnp.sin(ang)


def _rope(x, cos, sin):
    x1, x2 = jnp.split(x, 2, -1)
    c, s = cos[None, :, None], sin[None, :, None]
    return jnp.concatenate([x1 * c - x2 * s, x1 * s + x2 * c], -1)


def _token_mix(h, p, lb, rot, s0):
    b, t, _ = h.shape
    dt = h.dtype
    f32 = jnp.float32
    proj = h @ p['w_in']
    (dn_qkv, dn_z, dn_b, dn_a, rt_qkv, rt_g, hg_q, hg_f, hg_i, hg_g, merge) = jnp.split(proj, SPLIT_POINTS, axis=-1)
    heads = lambda a: a.reshape(b, t, N_HEADS, HEAD_DIM).astype(f32)
    s_dn, s_rt, s_hg = s0
    dn_qkv = jax.nn.silu(_short_conv(dn_qkv, p['dn_conv_w']))
    q, k, v = (heads(a) for a in jnp.split(dn_qkv, 3, -1))
    q = _l2norm(q) * HEAD_DIM ** -0.5
    k = _l2norm(k)
    beta = jax.nn.sigmoid(dn_b.astype(f32)).reshape(b, t, N_DIR, N_HEADS)
    g = -jnp.exp(p['dn_a_log'].astype(f32)) * jax.nn.softplus(
        dn_a.astype(f32).reshape(b, t, N_DIR, N_HEADS) + p['dn_dt_bias'].astype(f32))
    o_dn, f_dn = _two_way(_gated_delta_chunked,
                          [(q, k, v, g[:, :, d], beta[:, :, d]) for d in range(N_DIR)],
                          [(), ()], s_dn)
    o_dn = _rmsnorm(o_dn, p['branch_norm_w'][0]) * jax.nn.silu(heads(dn_z))
    q, k, v = (heads(a) for a in jnp.split(rt_qkv, 3, -1))
    if rot is not None:
        q, k = _rope(q, *rot), _rope(k, *rot)
    q = q * HEAD_DIM ** -0.5
    lg = -jnp.exp(p['rt_log_decay'].astype(f32))
    o_rt, f_rt = _two_way(_retention_chunked, [(q, k, v), (q, k, v)], [(lg[0],), (lg[1],)], s_rt)
    o_rt = _headnorm(o_rt, p['branch_norm_w'][1]) * jax.nn.silu(heads(rt_g))
    q = jax.nn.silu(heads(hg_q)) * HEAD_DIM ** -0.5
    lbh = lb.reshape(N_HEADS, HEAD_DIM)
    f = lbh + (1.0 - lbh) * jax.nn.sigmoid(hg_f.astype(f32).reshape(b, t, N_DIR, N_HEADS, HEAD_DIM))
    i_in = heads(hg_i)
    o_hg, f_hg = _two_way(_gla_chunked,
                          [(q, 1.0 - f[:, :, d], i_in, jnp.log(f[:, :, d])) for d in range(N_DIR)],
                          [(), ()], s_hg)
    o_hg = _rmsnorm(o_hg, p['branch_norm_w'][2]) * jax.nn.silu(heads(hg_g))
    branches = jnp.stack([o_dn, o_rt, o_hg], 2).reshape(b, t, 3, MIX_W).astype(dt)
    proj_b = jnp.einsum('btnm,nmd->btnd', branches, p['w_branch'])
    gates = jax.nn.sigmoid(merge.astype(f32).reshape(b, t, 3, D_MODEL)).astype(dt)
    mixed = jnp.sum(gates * proj_b, axis=2)
    return mixed @ p['w_out'], (f_dn, f_rt, f_hg)


def _peer(h, wq, subkeys, u_tab, v_tab):
    b, t, d = h.shape
    blocks = h.reshape(-1, PEER_BLOCK, d)

    def block(xb):
        qry = (xb @ wq).reshape(PEER_BLOCK, R_HEADS, 2, HALF_KEY)
        s = jnp.einsum('prsk,snk->prsn', qry, subkeys).astype(jnp.float32)
        s1, i1 = lax.top_k(s[:, :, 0], PEER_TOPK)
        s2, i2 = lax.top_k(s[:, :, 1], PEER_TOPK)
        cand_s = (s1[..., :, None] + s2[..., None, :]).reshape(PEER_BLOCK, R_HEADS, PEER_TOPK * PEER_TOPK)
        cand_i = (i1[..., :, None] * N_KEYS + i2[..., None, :]).reshape(PEER_BLOCK, R_HEADS, PEER_TOPK * PEER_TOPK)
        top_s, top_pos = lax.top_k(cand_s, PEER_TOPK)
        idx = jnp.take_along_axis(cand_i, top_pos, axis=-1)
        gate = jax.nn.softmax(top_s, axis=-1).astype(xb.dtype)
        u = jnp.take(u_tab, idx, axis=0)
        act = jax.nn.gelu(jnp.einsum('prkd,pd->prk', u, xb))
        v = jnp.take(v_tab, idx, axis=0)
        return jnp.einsum('prk,prkd->pd', gate * act, v)

    return lax.map(block, blocks).reshape(b, t, d)


def _layer(x, cond, p, lb, rot, s0):
    mod = jax.nn.silu(cond) @ p['ada_w'] + p['ada_b']
    sh1, sc1, g1, sh2, sc2, g2 = jnp.split(mod[:, None, :], 6, axis=-1)
    h = x * (1.0 + sc1) + sh1
    mix, finals = _token_mix(h, p, lb, rot, s0)
    x = _layernorm(ALPHA * x + g1 * mix, p['ln_g'][0], p['ln_b'][0])
    h = x * (1.0 + sc2) + sh2
    ffn = _peer(h, p['peer_wq'], p['peer_subkeys'], p['peer_u'], p['peer_v'])
    x = _layernorm(ALPHA * x + g2 * ffn, p['ln_g'][1], p['ln_b'][1])
    return x, finals


def setup_inputs(seed: int = 0) -> dict:
    key = jax.random.key(seed)
    ks = jax.random.split(key, 24)
    f32 = jnp.float32
    nrm = lambda k, shape, s: jax.random.normal(k, shape, f32) * s
    sshape = (DEC_BATCH, DEPTH, N_DIR, N_HEADS, HEAD_DIM, HEAD_DIM)
    dtv = jnp.exp(jax.random.uniform(ks[12], (DEPTH, N_DIR, N_HEADS), f32, math.log(1e-3), math.log(1e-1)))
    decay_base = jnp.log(-jnp.log(1.0 - 2.0 ** (-5.0 - jnp.arange(N_HEADS, dtype=f32))))
    return {
        'x_prompt': nrm(ks[0], (BATCH, SEQ, D_MODEL), 1.0),
        'x_sample': nrm(ks[1], (DEC_BATCH, DEC_SEQ, D_MODEL), 1.0),
        'state_delta': nrm(ks[2], sshape, 0.5),
        'state_ret': nrm(ks[3], sshape, 0.5),
        'state_hgrn': nrm(ks[4], sshape, 0.5),
        'c': nrm(ks[5], (DEC_BATCH, D_MODEL), 1.0),
        'c_ctx': nrm(ks[6], (D_MODEL,), 1.0),
        'ada_w': nrm(ks[7], (DEPTH, D_MODEL, 6 * D_MODEL), D_MODEL ** -0.5),
        'ada_b': nrm(ks[8], (DEPTH, 6 * D_MODEL), 0.02),
        'w_in': nrm(ks[9], (DEPTH, D_MODEL, IN_COLS), D_MODEL ** -0.5),
        'dn_conv_w': nrm(ks[10], (DEPTH, CONV_K, 3 * MIX_W), CONV_K ** -0.5),
        'dn_a_log': jnp.log(jax.random.uniform(ks[11], (DEPTH, N_DIR, N_HEADS), f32, 1.0, 16.0)),
        'dn_dt_bias': dtv + jnp.log(-jnp.expm1(-dtv)),
        'rt_log_decay': decay_base + nrm(ks[13], (DEPTH, N_DIR, N_HEADS), 0.05),
        'hg_lb_logits': nrm(ks[14], (DEPTH, MIX_W), 0.5),
        'branch_norm_w': 1.0 + nrm(ks[15], (DEPTH, 3, HEAD_DIM), 0.02),
        'w_branch': nrm(ks[16], (DEPTH, 3, MIX_W, D_MODEL), MIX_W ** -0.5),
        'w_out': nrm(ks[17], (DEPTH, D_MODEL, D_MODEL), INIT_BETA * D_MODEL ** -0.5),
        'ln_g': 1.0 + nrm(ks[18], (DEPTH, 2, D_MODEL), 0.02),
        'ln_b': nrm(ks[19], (DEPTH, 2, D_MODEL), 0.02),
        'peer_wq': nrm(ks[20], (DEPTH, D_MODEL, R_HEADS * KEY_DIM), D_MODEL ** -0.5),
        'peer_subkeys': nrm(ks[21], (DEPTH, 2, N_KEYS, HALF_KEY), HALF_KEY ** -0.5),
        'peer_u': nrm(ks[22], (DEPTH, N_EXPERTS, D_MODEL), D_MODEL ** -0.5),
        'peer_v': nrm(ks[23], (DEPTH, N_EXPERTS, D_MODEL), INIT_BETA),
    }


def reference(x_prompt, x_sample, state_delta, state_ret, state_hgrn, c, c_ctx, ada_w, ada_b, w_in,
              dn_conv_w, dn_a_log, dn_dt_bias, rt_log_decay, hg_lb_logits, branch_norm_w, w_branch, w_out,
              ln_g, ln_b, peer_wq, peer_subkeys, peer_u, peer_v):
    p_lb = jax.nn.softmax(hg_lb_logits.astype(jnp.float32), axis=0)
    lower_bounds = jnp.cumsum(p_lb, axis=0) - p_lb[0]

    def layer_params(l):
        return {'ada_w': ada_w[l], 'ada_b': ada_b[l], 'w_in': w_in[l], 'dn_conv_w': dn_conv_w[l],
                'dn_a_log': dn_a_log[l], 'dn_dt_bias': dn_dt_bias[l], 'rt_log_decay': rt_log_decay[l],
                'branch_norm_w': branch_norm_w[l], 'w_branch': w_branch[l], 'w_out': w_out[l],
                'ln_g': ln_g[l], 'ln_b': ln_b[l], 'peer_wq': peer_wq[l], 'peer_subkeys': peer_subkeys[l],
                'peer_u': peer_u[l], 'peer_v': peer_v[l]}

    zero = jnp.zeros((x_prompt.shape[0], N_DIR, N_HEADS, HEAD_DIM, HEAD_DIM), jnp.float32)
    y_prompt = x_prompt
    st_dn, st_rt, st_hg = [], [], []
    for l in range(DEPTH):
        y_prompt, (f_dn, f_rt, f_hg) = _layer(y_prompt, c_ctx[None, :], layer_params(l), lower_bounds[l],
                                              None, (zero, zero, zero))
        st_dn.append(f_dn)
        st_rt.append(f_rt)
        st_hg.append(f_hg)
    new_state_delta = jnp.stack(st_dn, axis=1).astype(x_prompt.dtype)
    new_state_ret = jnp.stack(st_rt, axis=1).astype(x_prompt.dtype)
    new_state_hgrn = jnp.stack(st_hg, axis=1).astype(x_prompt.dtype)

    rot = _grid_rotary(x_sample.shape[1])
    y_sample = x_sample
    for l in range(DEPTH):
        y_sample, _ = _layer(y_sample, c, layer_params(l), lower_bounds[l], rot,
                             (state_delta[:, l], state_ret[:, l], state_hgrn[:, l]))

    return (y_prompt, y_sample, new_state_delta, new_state_ret, new_state_hgrn)
```

```python
import functools
import math

import jax
import jax.numpy as jnp
from jax import lax
from jax.experimental import pallas as pl
from jax.experimental.pallas import tpu as pltpu

F32 = jnp.float32
BF16 = jnp.bfloat16
HI = lax.Precision.HIGHEST

D_MODEL = 1024
N_HEADS = 4
HEAD_DIM = 128
MIX_W = N_HEADS * HEAD_DIM
CHUNK = 64
SUB = 16
CONV_K = 5
GRID_W = 64
ROPE_BASE = 10000.0
N_KEYS = 128
R_HEADS = 8
HALF_KEY = 64
PEER_TOPK = 16
LN_EPS = 1e-5
NORM_EPS = 1e-6
LANES = 128
VMEM_LIMIT = 56 * 1024 * 1024

C_MERGE = 0
C_DNQKV = 3072
C_DNZ = 4608
C_RTQKV = 5120
C_RTG = 6656
C_HGQ = 7168
C_HGF = 7680
C_HGI = 8704
C_HGG = 9216
C_SMALL = 9728
N_COLS = 9856


def _dot(a, b, precision=None):
    return jnp.dot(a, b, precision=precision, preferred_element_type=F32)


def _dot_nt(a, b, precision=None):
    return lax.dot_general(a, b, (((1,), (1,)), ((), ())), precision=precision, preferred_element_type=F32)


def _dot_tn(a, b, precision=None):
    return lax.dot_general(a, b, (((0,), (0,)), ((), ())), precision=precision, preferred_element_type=F32)


def _bf(x):
    return x.astype(BF16)


def _params(sem, limit=VMEM_LIMIT):
    return pltpu.CompilerParams(dimension_semantics=sem, vmem_limit_bytes=limit)


def _ada_body(c_ref, w_ref, b_ref, o_ref):
    c = c_ref[...]
    s = c * jax.nn.sigmoid(c)
    o_ref[0] = _dot(_bf(s), _bf(w_ref[0])) + b_ref[0]


def _ada(cond8, ada_w, ada_b):
    depth, d, n = ada_w.shape
    tn = 1536
    return pl.pallas_call(
        _ada_body,
        grid=(depth, n // tn),
        in_specs=[pl.BlockSpec((8, d), lambda l, j: (0, 0)),
                  pl.BlockSpec((1, d, tn), lambda l, j: (l, 0, j)),
                  pl.BlockSpec((1, 1, tn), lambda l, j: (l, 0, j))],
        out_specs=pl.BlockSpec((1, 8, tn), lambda l, j: (l, 0, j)),
        out_shape=jax.ShapeDtypeStruct((depth, 8, n), F32),
        compiler_params=_params(("parallel", "parallel")),
        name="ada_mod",
    )(cond8, ada_w, ada_b.reshape(depth, 1, n))


def _inproj_body(x_ref, sc_ref, sh_ref, w_ref, o_ref):
    h = x_ref[...] * (1.0 + sc_ref[0]) + sh_ref[0]
    o_ref[...] = _dot(_bf(h), w_ref[...])


def _inproj(x, sc, sh, w, rows_per_cond):
    nt, d = x.shape
    ncol = w.shape[1]
    tm, tn = 1024, 896
    cond = lambda i: (i * tm) // rows_per_cond
    return pl.pallas_call(
        _inproj_body,
        grid=(nt // tm, ncol // tn),
        in_specs=[pl.BlockSpec((tm, d), lambda i, j: (i, 0)),
                  pl.BlockSpec((1, 1, d), lambda i, j: (cond(i), 0, 0)),
                  pl.BlockSpec((1, 1, d), lambda i, j: (cond(i), 0, 0)),
                  pl.BlockSpec((d, tn), lambda i, j: (0, j))],
        out_specs=pl.BlockSpec((tm, tn), lambda i, j: (i, j)),
        out_shape=jax.ShapeDtypeStruct((nt, ncol), F32),
        compiler_params=_params(("parallel", "arbitrary")),
        name="in_proj",
    )(x, sc, sh, w)


def _scan_masks(d):
    ii = lax.broadcasted_iota(jnp.int32, (CHUNK, CHUNK), 0)
    jj = lax.broadcasted_iota(jnp.int32, (CHUNK, CHUNK), 1)
    sgn = 1 - 2 * d
    dist = (ii - jj) * sgn
    return sgn, d == 0, dist


def _row_block(nc):
    return lambda b, d, c: (b * nc + c + d * (nc - 1 - 2 * c), 0)


def _dir_row_block(nc):
    return lambda b, d, c: (d, b * nc + c + d * (nc - 1 - 2 * c), 0)


def _dn_body(q_ref, k_ref, v_ref, gb_ref, s0_ref, o_ref, sf_ref, s_ref):
    d = pl.program_id(1)
    c = pl.program_id(2)

    @pl.when(c == 0)
    def _():
        s_ref[...] = s0_ref[0, 0]

    _, fwd, dist = _scan_masks(d)
    incl = dist >= 0
    strict = dist > 0
    mtri = incl.astype(F32)
    mtri_t = (dist <= 0).astype(F32)
    ones = jnp.ones((CHUNK, CHUNK), F32)
    gb = gb_ref[0]
    gc_all = _dot(mtri, gb, HI)
    last_all = jnp.where(fwd, gc_all[CHUNK - 1:CHUNK, :], gc_all[0:1, :])
    for h in range(N_HEADS):
        hs = slice(h * HEAD_DIM, (h + 1) * HEAD_DIM)
        q = q_ref[:, hs]
        k = k_ref[:, hs]
        v = v_ref[:, hs]
        g_col = gc_all[:, h:h + 1]
        beta = gb[:, N_HEADS + h:N_HEADS + h + 1]
        g_row = _dot(ones, gb[:, h:h + 1] * mtri_t, HI)
        dmask = jnp.where(incl, jnp.exp(jnp.where(incl, g_col - g_row, 0.0)), 0.0)
        kb = k * beta
        low = jnp.where(strict, _dot_nt(_bf(kb), _bf(k)) * dmask, 0.0)
        x = jnp.concatenate([v * beta, kb * jnp.exp(g_col)], axis=1)
        p = low
        x = x - _dot(p, x, HI)
        for _ in range(5):
            p = _dot(p, p, HI)
            x = x + _dot(p, x, HI)
        u = x[:, :HEAD_DIM]
        w = x[:, HEAD_DIM:]
        s = s_ref[h]
        s_b = _bf(s)
        v_new = u - _dot(_bf(w), s_b)
        attn = _dot_nt(_bf(q), _bf(k)) * dmask
        o = _dot(_bf(q * jnp.exp(g_col)), s_b) + _dot(_bf(attn), _bf(v_new))
        g_last = last_all[:, h:h + 1]
        s_ref[h] = s * jnp.exp(g_last) + _dot_tn(_bf(k * jnp.exp(g_last - g_col)), _bf(v_new))
        o_ref[0, :, hs] = o

    @pl.when(c == pl.num_programs(2) - 1)
    def _():
        sf_ref[0, 0] = s_ref[...]


def _rt_body(lg_ref, q_ref, k_ref, v_ref, s0_ref, o_ref, sf_ref, s_ref):
    d = pl.program_id(1)
    c = pl.program_id(2)

    @pl.when(c == 0)
    def _():
        s_ref[...] = s0_ref[0, 0]

    sgn, _, dist = _scan_masks(d)
    incl = dist >= 0
    rel = dist.astype(F32)
    pcol = lax.broadcasted_iota(jnp.int32, (CHUNK, 1), 0)
    pos = (pcol * sgn + d * (CHUNK - 1)).astype(F32)
    for h in range(N_HEADS):
        hs = slice(h * HEAD_DIM, (h + 1) * HEAD_DIM)
        lg = lg_ref[d, h]
        q = q_ref[:, hs]
        k = k_ref[:, hs]
        v = v_ref[:, hs]
        decay = jnp.where(incl, jnp.exp(jnp.where(incl, rel * lg, 0.0)), 0.0)
        q_decay = jnp.exp((pos + 1.0) * lg)
        k_decay = jnp.exp((CHUNK - 1.0 - pos) * lg)
        c_decay = jnp.exp(jnp.full((1, 1), CHUNK, F32) * lg)
        s = s_ref[h]
        attn = _dot_nt(_bf(q), _bf(k)) * decay
        o = _dot(_bf(attn), _bf(v)) + _dot(_bf(q * q_decay), _bf(s))
        s_ref[h] = s * c_decay + _dot_tn(_bf(k * k_decay), _bf(v))
        o_ref[0, :, hs] = o

    @pl.when(c == pl.num_programs(2) - 1)
    def _():
        sf_ref[0, 0] = s_ref[...]


def _hg_body(q_ref, k_ref, lf_ref, v_ref, s0_ref, o_ref, sf_ref, s_ref):
    d = pl.program_id(1)
    c = pl.program_id(2)

    @pl.when(c == 0)
    def _():
        s_ref[...] = s0_ref[0, 0]

    sgn, fwd, dist = _scan_masks(d)
    mtri = (dist >= 0).astype(F32)
    rloc = lax.broadcasted_iota(jnp.int32, (SUB, 1), 0)
    jcol = lax.broadcasted_iota(jnp.int32, (SUB, CHUNK), 1)
    for h in range(N_HEADS):
        hs = slice(h * HEAD_DIM, (h + 1) * HEAD_DIM)
        q = q_ref[:, hs]
        k = k_ref[0, :, hs]
        v = v_ref[:, hs]
        gc = _dot(mtri, lf_ref[0, :, hs], HI)
        g_last = jnp.where(fwd, gc[CHUNK - 1:CHUNK, :], gc[0:1, :])
        st = s_ref[h]
        o_inter = _dot_nt(_bf(q * jnp.exp(gc)), _bf(st))
        v_b = _bf(v)
        for blk in range(CHUNK // SUB):
            r0 = blk * SUB
            rs = slice(r0, r0 + SUB)
            g_blk = gc[rs]
            q_blk = q[rs]
            ref = jnp.where(fwd, gc[r0:r0 + 1, :], gc[r0 + SUB - 1:r0 + SUB, :])
            q_t = q_blk * jnp.exp(g_blk - ref)
            k_t = k * jnp.exp(jnp.minimum(ref - gc, 0.0))
            a = _dot_nt(_bf(q_t), _bf(k_t))
            earlier = (2 * jcol - (2 * r0 + SUB - 1)) * sgn < -(SUB - 1)
            o_blk = _dot(_bf(jnp.where(earlier, a, 0.0)), v_b) + o_inter[rs]
            for jl in range(SUB):
                j = r0 + jl
                pj = q_blk * k[j:j + 1, :] * jnp.exp(jnp.minimum(g_blk - gc[j:j + 1, :], 0.0))
                aj = jnp.sum(pj, axis=1, keepdims=True)
                keep = (rloc - jl) * sgn >= 0
                o_blk = o_blk + jnp.where(keep, aj, 0.0) * v[j:j + 1, :]
            o_ref[0, rs, hs] = o_blk
        s_ref[h] = st * jnp.exp(g_last) + _dot_tn(v_b, _bf(k * jnp.exp(g_last - gc)))

    @pl.when(c == pl.num_programs(2) - 1)
    def _():
        sf_ref[0, 0] = s_ref[...]


def _scan_call(body, name, seq_inputs, s0, nb, t, smem_inputs=()):
    nc = t // CHUNK
    nt = nb * t
    in_specs = [pl.BlockSpec(memory_space=pltpu.SMEM) for _ in smem_inputs]
    args = list(smem_inputs)
    for arr, directional in seq_inputs:
        if directional:
            in_specs.append(pl.BlockSpec((1, CHUNK, arr.shape[-1]), _dir_row_block(nc)))
        else:
            in_specs.append(pl.BlockSpec((CHUNK, arr.shape[-1]), _row_block(nc)))
        args.append(arr)
    st_spec = pl.BlockSpec((1, 1, N_HEADS, HEAD_DIM, HEAD_DIM), lambda b, d, c: (b, d, 0, 0, 0))
    in_specs.append(st_spec)
    args.append(s0)
    return pl.pallas_call(
        body,
        grid=(nb, 2, nc),
        in_specs=in_specs,
        out_specs=[pl.BlockSpec((1, CHUNK, MIX_W), _dir_row_block(nc)), st_spec],
        out_shape=[jax.ShapeDtypeStruct((2, nt, MIX_W), F32),
                   jax.ShapeDtypeStruct((nb, 2, N_HEADS, HEAD_DIM, HEAD_DIM), F32)],
        scratch_shapes=[pltpu.VMEM((N_HEADS, HEAD_DIM, HEAD_DIM), F32)],
        compiler_params=_params(("parallel", "parallel", "arbitrary")),
        name=name,
    )(*args)


def _layernorm_rows(y, g, b):
    mu = jnp.mean(y, axis=-1, keepdims=True)
    yc = y - mu
    var = jnp.mean(yc * yc, axis=-1, keepdims=True)
    return yc * lax.rsqrt(var + LN_EPS) * g + b


def _silu(x):
    return x * jax.nn.sigmoid(x)


def _merge_body(alpha, odn_ref, ort_ref, ohg_ref, z_ref, rg_ref, hgg_ref, mg_ref, x_ref, g1_ref, bnw_ref,
                wb_ref, wo_ref, lng_ref, lnb_ref, o_ref):
    bnw = bnw_ref[...]

    def branch(o_ref2, gate_ref, n, centered):
        o = o_ref2[0] + o_ref2[1]
        parts = []
        for h in range(N_HEADS):
            hs = slice(h * HEAD_DIM, (h + 1) * HEAD_DIM)
            xh = o[:, hs]
            if centered:
                xh = xh - jnp.mean(xh, axis=-1, keepdims=True)
            ms = jnp.mean(xh * xh, axis=-1, keepdims=True)
            parts.append(xh * lax.rsqrt(ms + NORM_EPS) * bnw[n:n + 1, :])
        y = jnp.concatenate(parts, axis=1) * _silu(gate_ref[...])
        return _dot(_bf(y), wb_ref[n])

    mixed = None
    for n, (oref, gref, centered) in enumerate(((odn_ref, z_ref, False), (ort_ref, rg_ref, True),
                                                (ohg_ref, hgg_ref, False))):
        pb = branch(oref, gref, n, centered)
        gate = jax.nn.sigmoid(mg_ref[:, n * D_MODEL:(n + 1) * D_MODEL])
        mixed = gate * pb if mixed is None else mixed + gate * pb
    mix = _dot(_bf(mixed), wo_ref[...])
    y = alpha * x_ref[...] + g1_ref[0] * mix
    o_ref[...] = _layernorm_rows(y, lng_ref[...], lnb_ref[...])


def _merge(alpha, o_dn, o_rt, o_hg, proj, x, g1, bnw, wb, wo, lng, lnb, rows_per_cond):
    nt, d = x.shape
    tm = 256
    cond = lambda i: (i * tm) // rows_per_cond
    o_spec = pl.BlockSpec((2, tm, MIX_W), lambda i: (0, i, 0))
    col = lambda c0: pl.BlockSpec((tm, MIX_W), lambda i: (i, c0 // MIX_W))
    full = lambda a: pl.BlockSpec(a.shape, lambda i: (0,) * a.ndim)
    return pl.pallas_call(
        functools.partial(_merge_body, alpha),
        grid=(nt // tm,),
        in_specs=[o_spec, o_spec, o_spec, col(C_DNZ), col(C_RTG), col(C_HGG),
                  pl.BlockSpec((tm, 3 * D_MODEL), lambda i: (i, 0)),
                  pl.BlockSpec((tm, d), lambda i: (i, 0)),
                  pl.BlockSpec((1, 1, d), lambda i: (cond(i), 0, 0)),
                  full(bnw), full(wb), full(wo), full(lng), full(lnb)],
        out_specs=pl.BlockSpec((tm, d), lambda i: (i, 0)),
        out_shape=jax.ShapeDtypeStruct((nt, d), F32),
        compiler_params=_params(("parallel",)),
        name="merge_ln",
    )(o_dn, o_rt, o_hg, proj, proj, proj, proj, x, g1, bnw, wb, wo, lng, lnb)


def _top_values(s, k):
    n = s.shape[0]
    it = lax.broadcasted_iota(jnp.int32, s.shape, 0)
    out = []
    for _ in range(k):
        m = jnp.max(s, axis=0, keepdims=True)
        first = jnp.min(jnp.where(s == m, it, n), axis=0, keepdims=True)
        s = jnp.where(it == first, -jnp.inf, s)
        out.append(m)
    return jnp.concatenate(out, axis=0)


def _route_body(x_ref, sc_ref, sh_ref, wq_ref, sk_ref, h_ref, s1_ref, s2_ref, e1_ref, e2_ref, tau_ref):
    h = x_ref[...] * (1.0 + sc_ref[0]) + sh_ref[0]
    hb = _bf(h)
    h_ref[...] = hb
    qt = _dot_nt(wq_ref[...], hb)
    for r in range(R_HEADS):
        tops = []
        for half in range(2):
            r0 = (2 * r + half) * HALF_KEY
            s = _dot(sk_ref[half], _bf(qt[r0:r0 + HALF_KEY, :]))
            (s1_ref if half == 0 else s2_ref)[r] = s
            tops.append(_top_values(s, PEER_TOPK))
        cand = jnp.concatenate([tops[0][i:i + 1, :] + tops[1] for i in range(PEER_TOPK)], axis=0)
        top = _top_values(cand, PEER_TOPK)
        z = jnp.sum(jnp.exp(top - top[0:1, :]), axis=0, keepdims=True)
        e1_ref[r] = jnp.exp(s1_ref[r] - tops[0][0:1, :])
        e2_ref[r] = jnp.exp(s2_ref[r] - tops[1][0:1, :]) / z
        tau_ref[r] = jnp.broadcast_to(top[PEER_TOPK - 1:PEER_TOPK, :], tau_ref.shape[1:])


def _route(x, sc, sh, wq_t, sk, rows_per_cond):
    nt, d = x.shape
    tm = 256
    cond = lambda i: (i * tm) // rows_per_cond
    key_spec = pl.BlockSpec((R_HEADS, N_KEYS, tm), lambda i: (0, 0, i))
    key_shape = jax.ShapeDtypeStruct((R_HEADS, N_KEYS, nt), F32)
    return pl.pallas_call(
        _route_body,
        grid=(nt // tm,),
        in_specs=[pl.BlockSpec((tm, d), lambda i: (i, 0)),
                  pl.BlockSpec((1, 1, d), lambda i: (cond(i), 0, 0)),
                  pl.BlockSpec((1, 1, d), lambda i: (cond(i), 0, 0)),
                  pl.BlockSpec(wq_t.shape, lambda i: (0, 0)),
                  pl.BlockSpec(sk.shape, lambda i: (0, 0, 0))],
        out_specs=[pl.BlockSpec((tm, d), lambda i: (i, 0)), key_spec, key_spec, key_spec, key_spec,
                   pl.BlockSpec((R_HEADS, 8, tm), lambda i: (0, 0, i))],
        out_shape=[jax.ShapeDtypeStruct((nt, d), BF16), key_shape, key_shape, key_shape, key_shape,
                   jax.ShapeDtypeStruct((R_HEADS, 8, nt), F32)],
        compiler_params=_params(("parallel",)),
        name="peer_route",
    )(x, sc, sh, wq_t, sk)


def _gelu_tanh(x):
    return 0.5 * x * (1.0 + jnp.tanh(math.sqrt(2.0 / math.pi) * (x + 0.044715 * (x * x * x))))


def _expert_body(alpha, te, h_ref, s1_ref, s2_ref, e1_ref, e2_ref, tau_ref, u_ref, vt_ref, x_ref, g2_ref,
                 lng_ref, lnb_ref, o_ref, acc_ref, gt_ref):
    e = pl.program_id(1)

    @pl.when(e == 0)
    def _():
        acc_ref[...] = jnp.zeros_like(acc_ref)

    tm = h_ref.shape[0]
    at = _dot_nt(u_ref[...], h_ref[...])
    n_slab = te // N_KEYS
    a0 = pl.multiple_of(e * n_slab, n_slab)
    for al in range(n_slab):
        for pc in range(tm // LANES):
            ps = slice(pc * LANES, (pc + 1) * LANES)
            w = jnp.zeros((N_KEYS, LANES), F32)
            for r in range(R_HEADS):
                s1_row = s1_ref[r, pl.ds(a0, n_slab), ps][al:al + 1, :]
                e1_row = e1_ref[r, pl.ds(a0, n_slab), ps][al:al + 1, :]
                ssum = s2_ref[r, :, ps] + s1_row
                sel = jnp.where(ssum >= tau_ref[r, 0:1, ps], e2_ref[r, :, ps], 0.0)
                w = w + sel * e1_row
            act = _gelu_tanh(at[al * N_KEYS:(al + 1) * N_KEYS, ps])
            gt_ref[al * N_KEYS:(al + 1) * N_KEYS, ps] = _bf(w * act)
    acc_ref[...] += _dot(vt_ref[...], gt_ref[...])

    @pl.when(e == pl.num_programs(1) - 1)
    def _():
        y = alpha * x_ref[...] + g2_ref[0] * acc_ref[...].T
        o_ref[...] = _layernorm_rows(y, lng_ref[...], lnb_ref[...])


def _experts(alpha, hb, s1, s2, e1, e2, tau, u_b, vt_b, x, g2, lng, lnb, rows_per_cond):
    nt, d = x.shape
    ne = u_b.shape[0]
    tm, te = 512, 1024
    cond = lambda i: (i * tm) // rows_per_cond
    key_spec = pl.BlockSpec((R_HEADS, N_KEYS, tm), lambda i, e: (0, 0, i))
    return pl.pallas_call(
        functools.partial(_expert_body, alpha, te),
        grid=(nt // tm, ne // te),
        in_specs=[pl.BlockSpec((tm, d), lambda i, e: (i, 0)),
                  key_spec, key_spec, key_spec, key_spec,
                  pl.BlockSpec((R_HEADS, 8, tm), lambda i, e: (0, 0, i)),
                  pl.BlockSpec((te, d), lambda i, e: (e, 0)),
                  pl.BlockSpec((d, te), lambda i, e: (0, e)),
                  pl.BlockSpec((tm, d), lambda i, e: (i, 0)),
                  pl.BlockSpec((1, 1, d), lambda i, e: (cond(i), 0, 0)),
                  pl.BlockSpec(lng.shape, lambda i, e: (0, 0)),
                  pl.BlockSpec(lnb.shape, lambda i, e: (0, 0))],
        out_specs=pl.BlockSpec((tm, d), lambda i, e: (i, 0)),
        out_shape=jax.ShapeDtypeStruct((nt, d), F32),
        scratch_shapes=[pltpu.VMEM((d, tm), F32), pltpu.VMEM((te, tm), BF16)],
        compiler_params=_params(("parallel", "arbitrary")),
        name="peer_experts",
    )(hb, s1, s2, e1, e2, tau, u_b, vt_b, x, g2, lng, lnb)


def _l2norm(x):
    return x * lax.rsqrt(jnp.sum(jnp.square(x), -1, keepdims=True) + NORM_EPS)


def _grid_rotary(n_tok):
    rows = n_tok // GRID_W
    r = jnp.repeat(jnp.arange(rows), GRID_W).astype(F32)
    col = jnp.tile(jnp.arange(GRID_W), rows).astype(F32)
    n_freq = HEAD_DIM // 4
    inv = ROPE_BASE ** (-jnp.arange(n_freq, dtype=F32) / n_freq)
    ang = jnp.concatenate([r[:, None] * inv, col[:, None] * inv], -1)
    return jnp.cos(ang), jnp.sin(ang)


def _rope(x, cos, sin):
    x1, x2 = jnp.split(x, 2, -1)
    c, s = cos[None, :, None], sin[None, :, None]
    return jnp.concatenate([x1 * c - x2 * s, x1 * s + x2 * c], -1)


def _prep_group(proj, nb, t, conv_w, a_log, dt_bias, lb, rot):
    nt = nb * t
    heads = lambda a: a.reshape(nb, t, N_HEADS, HEAD_DIM)
    flat = lambda a: a.reshape(nt, MIX_W)
    x = proj[:, C_DNQKV:C_DNQKV + 3 * MIX_W].reshape(nb, t, 3 * MIX_W)
    xp = jnp.pad(x, ((0, 0), (CONV_K // 2, CONV_K // 2), (0, 0)))
    y = sum(xp[:, i:i + t, :] * conv_w[i][None, None, :] for i in range(CONV_K))
    y = jax.nn.silu(y)
    q, k, v = (heads(a) for a in jnp.split(y, 3, -1))
    q = _l2norm(q) * HEAD_DIM ** -0.5
    k = _l2norm(k)
    small = proj[:, C_SMALL:C_SMALL + 16]
    beta = jax.nn.sigmoid(small[:, 0:8]).reshape(nt, 2, N_HEADS)
    g = -jnp.exp(a_log) * jax.nn.softplus(small[:, 8:16].reshape(nt, 2, N_HEADS) + dt_bias)
    gb = jnp.concatenate([g, beta, jnp.zeros((nt, 2, LANES - 2 * N_HEADS), F32)], axis=-1)
    dn = (flat(q), flat(k), flat(v), jnp.swapaxes(gb, 0, 1))
    q, k, v = (heads(a) for a in jnp.split(proj[:, C_RTQKV:C_RTQKV + 3 * MIX_W], 3, -1))
    if rot is not None:
        q, k = _rope(q, *rot), _rope(k, *rot)
    rt = (flat(q * HEAD_DIM ** -0.5), flat(k), flat(v))
    q = jax.nn.silu(proj[:, C_HGQ:C_HGQ + MIX_W]) * HEAD_DIM ** -0.5
    f = lb + (1.0 - lb) * jax.nn.sigmoid(proj[:, C_HGF:C_HGF + 2 * MIX_W].reshape(nt, 2, MIX_W))
    f = jnp.swapaxes(f, 0, 1)
    hg = (q, 1.0 - f, jnp.log(f), proj[:, C_HGI:C_HGI + MIX_W])
    return dn, rt, hg


def _mix_group(proj, nb, t, conv_w, a_log, dt_bias, lg, lb, rot, s_dn, s_rt, s_hg):
    dn, rt, hg = _prep_group(proj, nb, t, conv_w, a_log, dt_bias, lb, rot)
    o_dn, f_dn = _scan_call(_dn_body, "scan_delta",
                            [(dn[0], False), (dn[1], False), (dn[2], False), (dn[3], True)], s_dn, nb, t)
    o_rt, f_rt = _scan_call(_rt_body, "scan_ret",
                            [(rt[0], False), (rt[1], False), (rt[2], False)], s_rt, nb, t, smem_inputs=(lg,))
    o_hg, f_hg = _scan_call(_hg_body, "scan_hgrn",
                            [(hg[0], False), (hg[1], True), (hg[2], True), (hg[3], False)],
                            jnp.swapaxes(s_hg, -1, -2), nb, t)
    return (o_dn, o_rt, o_hg), (f_dn, f_rt, jnp.swapaxes(f_hg, -1, -2))


def kernel(x_prompt, x_sample, state_delta, state_ret, state_hgrn, c, c_ctx, ada_w, ada_b, w_in, dn_conv_w,
           dn_a_log, dn_dt_bias, rt_log_decay, hg_lb_logits, branch_norm_w, w_branch, w_out, ln_g, ln_b,
           peer_wq, peer_subkeys, peer_u, peer_v):
    nb1, t1, d = x_prompt.shape
    nb2, t2, _ = x_sample.shape
    depth = w_in.shape[0]
    alpha = (2 * depth) ** 0.25
    n1 = nb1 * t1
    assert n1 == t2, "modulation rows are looked up per block of t2 tokens"
    rows_per_cond = t2

    p_lb = jax.nn.softmax(hg_lb_logits.astype(F32), axis=0)
    lower_bounds = jnp.cumsum(p_lb, axis=0) - p_lb[0]
    rot = _grid_rotary(t2)

    cond8 = jnp.zeros((8, d), F32).at[0].set(c_ctx).at[1:1 + nb2].set(c)
    mod = _ada(cond8, ada_w, ada_b)

    x = jnp.concatenate([x_prompt.reshape(n1, d), x_sample.reshape(nb2 * t2, d)], axis=0)
    zero_state = jnp.zeros((nb1, 2, N_HEADS, HEAD_DIM, HEAD_DIM), F32)
    finals = []
    for l in range(depth):
        sh1, sc1, g1, sh2, sc2, g2 = (m.reshape(8, 1, d) for m in jnp.split(mod[l], 6, axis=-1))
        wl = w_in[l]
        seg = lambda a, b: wl[:, a:b]
        small = jnp.concatenate([seg(4 * MIX_W, 4 * MIX_W + 16), jnp.zeros((d, LANES - 16), F32)], axis=1)
        o0 = 4 * MIX_W + 16
        w_cat = _bf(jnp.concatenate([seg(o0 + 9 * MIX_W, o0 + 9 * MIX_W + 3 * d),
                                     seg(0, 4 * MIX_W),
                                     seg(o0, o0 + 9 * MIX_W),
                                     small], axis=1))
        proj = _inproj(x, sc1, sh1, w_cat, rows_per_cond)
        lg = -jnp.exp(rt_log_decay[l].astype(F32))
        common = (dn_conv_w[l], dn_a_log[l], dn_dt_bias[l], lg, lower_bounds[l])
        o1, f1 = _mix_group(proj[:n1], nb1, t1, *common, None, zero_state, zero_state, zero_state)
        o2, _ = _mix_group(proj[n1:], nb2, t2, *common, rot, state_delta[:, l], state_ret[:, l], state_hgrn[:, l])
        finals.append(f1)
        o_dn, o_rt, o_hg = (jnp.concatenate([a, b], axis=1) for a, b in zip(o1, o2))
        x = _merge(alpha, o_dn, o_rt, o_hg, proj, x, g1, branch_norm_w[l], _bf(w_branch[l]), _bf(w_out[l]),
                   ln_g[l, 0:1], ln_b[l, 0:1], rows_per_cond)
        hb, s1, s2, e1, e2, tau = _route(x, sc2, sh2, _bf(peer_wq[l].T), _bf(peer_subkeys[l]), rows_per_cond)
        x = _experts(alpha, hb, s1, s2, e1, e2, tau, _bf(peer_u[l]), _bf(peer_v[l].T), x, g2,
                     ln_g[l, 1:2], ln_b[l, 1:2], rows_per_cond)

    y_prompt = x[:n1].reshape(nb1, t1, d)
    y_sample = x[n1:].reshape(nb2, t2, d)
    new_states = tuple(jnp.stack([f[i] for f in finals], axis=1) for i in range(3))
    return (y_prompt, y_sample) + new_states
```

```python
import functools
import math

import jax
import jax.numpy as jnp
from jax import lax
from jax.experimental import pallas as pl
from jax.experimental.pallas import tpu as pltpu

F32 = jnp.float32
BF16 = jnp.bfloat16

D_MODEL = 1024
N_HEADS = 4
HEAD_DIM = 128
MIX_W = N_HEADS * HEAD_DIM
N_DIR = 2
CHUNK = 64
SUB = 16
CONV_K = 5
GRID_W = 64
ROPE_BASE = 10000.0
N_KEYS = 128
R_HEADS = 8
HALF_KEY = 64
PEER_TOPK = 16
LN_EPS = 1e-5
NORM_EPS = 1e-6
LANES = 128
VMEM_LIMIT = 56 * 1024 * 1024

C_MERGE = 0
C_DNQKV = 3072
C_DNZ = 4608
C_RTQKV = 5120
C_RTG = 6656
C_HGQ = 7168
C_HGF = 7680
C_HGI = 8704
C_HGG = 9216
C_SMALL = 9728
N_COLS = 9856


def _dot(a, b):
    return jnp.dot(a, b, preferred_element_type=F32)


def _dot_nt(a, b):
    return lax.dot_general(a, b, (((1,), (1,)), ((), ())), preferred_element_type=F32)


def _dot_tn(a, b):
    return lax.dot_general(a, b, (((0,), (0,)), ((), ())), preferred_element_type=F32)


def _bf(x):
    return x.astype(BF16)


def _params(sem, limit=VMEM_LIMIT):
    return pltpu.CompilerParams(dimension_semantics=sem, vmem_limit_bytes=limit)


def _ada_body(c_ref, w_ref, b_ref, o_ref):
    c = c_ref[...]
    s = c * jax.nn.sigmoid(c)
    o_ref[0] = _dot(_bf(s), _bf(w_ref[0])) + b_ref[0]


def _ada(cond8, ada_w, ada_b):
    depth, d, n = ada_w.shape
    tn = 1536
    return pl.pallas_call(
        _ada_body,
        grid=(depth, n // tn),
        in_specs=[pl.BlockSpec((8, d), lambda l, j: (0, 0)),
                  pl.BlockSpec((1, d, tn), lambda l, j: (l, 0, j)),
                  pl.BlockSpec((1, 1, tn), lambda l, j: (l, 0, j))],
        out_specs=pl.BlockSpec((1, 8, tn), lambda l, j: (l, 0, j)),
        out_shape=jax.ShapeDtypeStruct((depth, 8, n), F32),
        compiler_params=_params(("parallel", "parallel")),
        name="ada_mod",
    )(cond8, ada_w, ada_b.reshape(depth, 1, n))


def _inproj_body(x_ref, sc_ref, sh_ref, w_ref, o_ref):
    h = x_ref[...] * (1.0 + sc_ref[0]) + sh_ref[0]
    o_ref[...] = _dot(_bf(h), w_ref[...])


def _inproj(x, sc, sh, w, rows_per_cond):
    nt, d = x.shape
    ncol = w.shape[1]
    tm, tn = 1024, 896
    cond = lambda i: (i * tm) // rows_per_cond
    return pl.pallas_call(
        _inproj_body,
        grid=(nt // tm, ncol // tn),
        in_specs=[pl.BlockSpec((tm, d), lambda i, j: (i, 0)),
                  pl.BlockSpec((1, 1, d), lambda i, j: (cond(i), 0, 0)),
                  pl.BlockSpec((1, 1, d), lambda i, j: (cond(i), 0, 0)),
                  pl.BlockSpec((d, tn), lambda i, j: (0, j))],
        out_specs=pl.BlockSpec((tm, tn), lambda i, j: (i, j)),
        out_shape=jax.ShapeDtypeStruct((nt, ncol), F32),
        compiler_params=_params(("parallel", "arbitrary")),
        name="in_proj",
    )(x, sc, sh, w)


def _split2(a):
    hi = _bf(a)
    return hi, _bf(a - hi.astype(F32))


def _dot3(ah, al, bh, bl):
    return _dot(ah, bh) + (_dot(ah, bl) + _dot(al, bh))


def _cumsum_rows(mask_bf, x):
    h1 = _bf(x)
    r1 = x - h1.astype(F32)
    h2 = _bf(r1)
    h3 = _bf(r1 - h2.astype(F32))
    return _dot(mask_bf, h1) + (_dot(mask_bf, h2) + _dot(mask_bf, h3))


def _chunk_dist(d):
    ii = lax.broadcasted_iota(jnp.int32, (CHUNK, CHUNK), 0)
    jj = lax.broadcasted_iota(jnp.int32, (CHUNK, CHUNK), 1)
    return ii - jj if d == 0 else jj - ii


def _last_row(x, d):
    return x[CHUNK - 1:CHUNK, :] if d == 0 else x[0:1, :]


def _head(h):
    return slice(h * HEAD_DIM, (h + 1) * HEAD_DIM)


_CHAINS = [(d, h) for d in range(N_DIR) for h in range(N_HEADS)]


def _dn_body(qf_ref, kf_ref, vf_ref, gf_ref, qb_ref, kb_ref, vb_ref, gb_ref, s0_ref, of_ref, ob_ref, sf_ref, s_ref):
    c = pl.program_id(1)

    @pl.when(c == 0)
    def _():
        s_ref[...] = s0_ref[0]

    refs = ((qf_ref, kf_ref, vf_ref, gf_ref, of_ref), (qb_ref, kb_ref, vb_ref, gb_ref, ob_ref))
    incl, strict, gbs, gcs, gcts = [], [], [], [], []
    for d in range(N_DIR):
        dist = _chunk_dist(d)
        incl.append(dist >= 0)
        strict.append(dist > 0)
        g = refs[d][3][0]
        gc = _cumsum_rows(incl[d].astype(BF16), g)
        gbs.append(g)
        gcs.append(gc)
        gcts.append(gc.T)
    q, k, v, g_col, beta, dmask = {}, {}, {}, {}, {}, {}
    for n in _CHAINS:
        d, h = n
        q[n] = refs[d][0][:, _head(h)]
        k[n] = refs[d][1][:, _head(h)]
        v[n] = refs[d][2][:, _head(h)]
        g_col[n] = gcs[d][:, h:h + 1]
        beta[n] = gbs[d][:, N_HEADS + h:N_HEADS + h + 1]
        diff = g_col[n] - gcts[d][h:h + 1, :]
        dmask[n] = jnp.where(incl[d], jnp.exp(jnp.where(incl[d], diff, 0.0)), 0.0)
    kb = {n: k[n] * beta[n] for n in _CHAINS}
    k_b = {n: _bf(k[n]) for n in _CHAINS}
    kk = {n: _dot_nt(_bf(kb[n]), k_b[n]) for n in _CHAINS}
    qk = {n: _dot_nt(_bf(q[n]), k_b[n]) for n in _CHAINS}
    p = {n: _split2(jnp.where(strict[n[0]], kk[n] * dmask[n], 0.0)) for n in _CHAINS}
    x = {n: jnp.concatenate([v[n] * beta[n], kb[n] * jnp.exp(g_col[n])], axis=1) for n in _CHAINS}
    for stage in range(6):
        xs = {n: _split2(x[n]) for n in _CHAINS}
        px = {n: _dot3(*p[n], *xs[n]) for n in _CHAINS}
        x = {n: (x[n] - px[n]) if stage == 0 else (x[n] + px[n]) for n in _CHAINS}
        if stage < 5:
            p = {n: _split2(_dot3(*p[n], *p[n])) for n in _CHAINS}
    s = {n: s_ref[n[0], n[1]] for n in _CHAINS}
    s_b = {n: _bf(s[n]) for n in _CHAINS}
    ws = {n: _dot(_bf(x[n][:, HEAD_DIM:]), s_b[n]) for n in _CHAINS}
    qs = {n: _dot(_bf(q[n] * jnp.exp(g_col[n])), s_b[n]) for n in _CHAINS}
    v_new = {n: _bf(x[n][:, :HEAD_DIM] - ws[n]) for n in _CHAINS}
    av = {n: _dot(_bf(qk[n] * dmask[n]), v_new[n]) for n in _CHAINS}
    g_last = {n: _last_row(gcs[n[0]], n[0])[:, n[1]:n[1] + 1] for n in _CHAINS}
    kv = {n: _dot_tn(_bf(k[n] * jnp.exp(g_last[n] - g_col[n])), v_new[n]) for n in _CHAINS}
    for n in _CHAINS:
        d, h = n
        s_ref[d, h] = s[n] * jnp.exp(g_last[n]) + kv[n]
        refs[d][4][:, _head(h)] = qs[n] + av[n]

    @pl.when(c == pl.num_programs(1) - 1)
    def _():
        sf_ref[0] = s_ref[...]


def _rt_body(lg_ref, qf_ref, kf_ref, vf_ref, qb_ref, kb_ref, vb_ref, s0_ref, of_ref, ob_ref, sf_ref, s_ref):
    c = pl.program_id(1)

    @pl.when(c == 0)
    def _():
        s_ref[...] = s0_ref[0]

    refs = ((qf_ref, kf_ref, vf_ref, of_ref), (qb_ref, kb_ref, vb_ref, ob_ref))
    pcol = lax.broadcasted_iota(jnp.int32, (CHUNK, 1), 0)
    q, k, v, qk, decay, q_dec, k_dec, c_dec = {}, {}, {}, {}, {}, {}, {}, {}
    for n in _CHAINS:
        d, h = n
        dist = _chunk_dist(d)
        pos = (pcol if d == 0 else CHUNK - 1 - pcol).astype(F32)
        lg = lg_ref[d, h]
        q[n] = refs[d][0][:, _head(h)]
        k[n] = refs[d][1][:, _head(h)]
        v[n] = _bf(refs[d][2][:, _head(h)])
        decay[n] = jnp.where(dist >= 0, jnp.exp(jnp.maximum(dist, 0).astype(F32) * lg), 0.0)
        q_dec[n] = jnp.exp((pos + 1.0) * lg)
        k_dec[n] = jnp.exp((CHUNK - 1.0 - pos) * lg)
        c_dec[n] = jnp.exp(jnp.full((1, 1), CHUNK, F32) * lg)
    qk = {n: _dot_nt(_bf(q[n]), _bf(k[n])) for n in _CHAINS}
    s = {n: s_ref[n[0], n[1]] for n in _CHAINS}
    qs = {n: _dot(_bf(q[n] * q_dec[n]), _bf(s[n])) for n in _CHAINS}
    av = {n: _dot(_bf(qk[n] * decay[n]), v[n]) for n in _CHAINS}
    kv = {n: _dot_tn(_bf(k[n] * k_dec[n]), v[n]) for n in _CHAINS}
    for n in _CHAINS:
        d, h = n
        s_ref[d, h] = s[n] * c_dec[n] + kv[n]
        refs[d][3][:, _head(h)] = av[n] + qs[n]

    @pl.when(c == pl.num_programs(1) - 1)
    def _():
        sf_ref[0] = s_ref[...]


def _hg_body(qf_ref, kf_ref, lf_ref, vf_ref, qb_ref, kb_ref, lb_ref, vb_ref, s0_ref, of_ref, ob_ref, sf_ref,
             s_ref):
    c = pl.program_id(1)

    @pl.when(c == 0)
    def _():
        s_ref[...] = s0_ref[0]

    refs = ((qf_ref, kf_ref, lf_ref, vf_ref, of_ref), (qb_ref, kb_ref, lb_ref, vb_ref, ob_ref))
    rloc = lax.broadcasted_iota(jnp.int32, (SUB, 1), 0)
    jcol = lax.broadcasted_iota(jnp.int32, (SUB, CHUNK), 1)
    incl_bf = [(_chunk_dist(d) >= 0).astype(BF16) for d in range(N_DIR)]
    q, k, v, gc = {}, {}, {}, {}
    for n in _CHAINS:
        d, h = n
        q[n] = refs[d][0][:, _head(h)]
        k[n] = refs[d][1][0, :, _head(h)]
        v[n] = refs[d][3][:, _head(h)]
    for n in _CHAINS:
        gc[n] = _cumsum_rows(incl_bf[n[0]], refs[n[0]][2][0, :, _head(n[1])])
    st = {n: s_ref[n[0], n[1]] for n in _CHAINS}
    v_b = {n: _bf(v[n]) for n in _CHAINS}
    o_inter = {n: _dot_nt(_bf(q[n] * jnp.exp(gc[n])), _bf(st[n])) for n in _CHAINS}
    g_last = {n: _last_row(gc[n], n[0]) for n in _CHAINS}
    kv = {n: _dot_tn(v_b[n], _bf(k[n] * jnp.exp(g_last[n] - gc[n]))) for n in _CHAINS}
    for blk in range(CHUNK // SUB):
        r0 = blk * SUB
        rs = slice(r0, r0 + SUB)
        a = {}
        for n in _CHAINS:
            d = n[0]
            ref = gc[n][r0:r0 + 1, :] if d == 0 else gc[n][r0 + SUB - 1:r0 + SUB, :]
            q_t = q[n][rs] * jnp.exp(gc[n][rs] - ref)
            k_t = k[n] * jnp.exp(jnp.minimum(ref - gc[n], 0.0))
            earlier = (jcol < r0) if d == 0 else (jcol >= r0 + SUB)
            a[n] = _bf(jnp.where(earlier, _dot_nt(_bf(q_t), _bf(k_t)), 0.0))
        o_off = {n: _dot(a[n], v_b[n]) for n in _CHAINS}
        for n in _CHAINS:
            d, h = n
            o_blk = o_off[n] + o_inter[n][rs]
            g_blk = gc[n][rs]
            q_blk = q[n][rs]
            for jl in range(SUB):
                j = r0 + jl
                pj = q_blk * k[n][j:j + 1, :] * jnp.exp(jnp.minimum(g_blk - gc[n][j:j + 1, :], 0.0))
                aj = jnp.sum(pj, axis=1, keepdims=True)
                keep = (rloc >= jl) if d == 0 else (rloc <= jl)
                o_blk = o_blk + jnp.where(keep, aj, 0.0) * v[n][j:j + 1, :]
            refs[d][4][rs, _head(h)] = o_blk
    for n in _CHAINS:
        s_ref[n[0], n[1]] = st[n] * jnp.exp(g_last[n]) + kv[n]

    @pl.when(c == pl.num_programs(1) - 1)
    def _():
        sf_ref[0] = s_ref[...]


def _scan_call(body, name, seq_inputs, s0, nb, t, smem_inputs=()):
    nc = t // CHUNK
    nt = nb * t
    fwd = lambda b, c: b * nc + c
    bwd = lambda b, c: b * nc + nc - 1 - c
    in_specs = [pl.BlockSpec(memory_space=pltpu.SMEM) for _ in smem_inputs]
    args = list(smem_inputs)
    for d, blk in enumerate((fwd, bwd)):
        for arr, directional in seq_inputs:
            if directional:
                in_specs.append(pl.BlockSpec((1, CHUNK, arr.shape[-1]), lambda b, c, d=d, blk=blk: (d, blk(b, c), 0)))
            else:
                in_specs.append(pl.BlockSpec((CHUNK, arr.shape[-1]), lambda b, c, blk=blk: (blk(b, c), 0)))
            args.append(arr)
    st_spec = pl.BlockSpec((1, N_DIR, N_HEADS, HEAD_DIM, HEAD_DIM), lambda b, c: (b, 0, 0, 0, 0))
    in_specs.append(st_spec)
    args.append(s0)
    o_shape = jax.ShapeDtypeStruct((nt, MIX_W), F32)
    return pl.pallas_call(
        body,
        grid=(nb, nc),
        in_specs=in_specs,
        out_specs=[pl.BlockSpec((CHUNK, MIX_W), lambda b, c: (fwd(b, c), 0)),
                   pl.BlockSpec((CHUNK, MIX_W), lambda b, c: (bwd(b, c), 0)), st_spec],
        out_shape=[o_shape, o_shape, jax.ShapeDtypeStruct((nb, N_DIR, N_HEADS, HEAD_DIM, HEAD_DIM), F32)],
        scratch_shapes=[pltpu.VMEM((N_DIR, N_HEADS, HEAD_DIM, HEAD_DIM), F32)],
        compiler_params=_params(("parallel", "arbitrary")),
        name=name,
    )(*args)


def _layernorm_rows(y, g, b):
    mu = jnp.mean(y, axis=-1, keepdims=True)
    yc = y - mu
    var = jnp.mean(yc * yc, axis=-1, keepdims=True)
    return yc * lax.rsqrt(var + LN_EPS) * g + b


def _silu(x):
    return x * jax.nn.sigmoid(x)


def _merge_body(alpha, odn_f, odn_b, ort_f, ort_b, ohg_f, ohg_b, z_ref, rg_ref, hgg_ref, mg_ref, x_ref, g1_ref,
                bnw_ref, wb_ref, wo_ref, lng_ref, lnb_ref, o_ref):
    bnw = bnw_ref[...]

    def branch(o_pair, gate_ref, n, centered):
        o = o_pair[0][...] + o_pair[1][...]
        parts = []
        for h in range(N_HEADS):
            hs = slice(h * HEAD_DIM, (h + 1) * HEAD_DIM)
            xh = o[:, hs]
            if centered:
                xh = xh - jnp.mean(xh, axis=-1, keepdims=True)
            ms = jnp.mean(xh * xh, axis=-1, keepdims=True)
            parts.append(xh * lax.rsqrt(ms + NORM_EPS) * bnw[n:n + 1, :])
        y = jnp.concatenate(parts, axis=1) * _silu(gate_ref[...])
        return _dot(_bf(y), wb_ref[n])

    mixed = None
    for n, (oref, gref, centered) in enumerate((((odn_f, odn_b), z_ref, False), ((ort_f, ort_b), rg_ref, True),
                                                ((ohg_f, ohg_b), hgg_ref, False))):
        pb = branch(oref, gref, n, centered)
        gate = jax.nn.sigmoid(mg_ref[:, n * D_MODEL:(n + 1) * D_MODEL])
        mixed = gate * pb if mixed is None else mixed + gate * pb
    mix = _dot(_bf(mixed), wo_ref[...])
    y = alpha * x_ref[...] + g1_ref[0] * mix
    o_ref[...] = _layernorm_rows(y, lng_ref[...], lnb_ref[...])


def _merge(alpha, o_dn, o_rt, o_hg, proj, x, g1, bnw, wb, wo, lng, lnb, rows_per_cond):
    nt, d = x.shape
    tm = 256
    cond = lambda i: (i * tm) // rows_per_cond
    o_spec = pl.BlockSpec((tm, MIX_W), lambda i: (i, 0))
    col = lambda c0: pl.BlockSpec((tm, MIX_W), lambda i: (i, c0 // MIX_W))
    full = lambda a: pl.BlockSpec(a.shape, lambda i: (0,) * a.ndim)
    return pl.pallas_call(
        functools.partial(_merge_body, alpha),
        grid=(nt // tm,),
        in_specs=[o_spec] * 6 + [col(C_DNZ), col(C_RTG), col(C_HGG),
                  pl.BlockSpec((tm, 3 * D_MODEL), lambda i: (i, 0)),
                  pl.BlockSpec((tm, d), lambda i: (i, 0)),
                  pl.BlockSpec((1, 1, d), lambda i: (cond(i), 0, 0)),
                  full(bnw), full(wb), full(wo), full(lng), full(lnb)],
        out_specs=pl.BlockSpec((tm, d), lambda i: (i, 0)),
        out_shape=jax.ShapeDtypeStruct((nt, d), F32),
        compiler_params=_params(("parallel",)),
        name="merge_ln",
    )(*o_dn, *o_rt, *o_hg, proj, proj, proj, proj, x, g1, bnw, wb, wo, lng, lnb)


def _mx(a, b):
    if a is None:
        return b
    return a if b is None else jnp.maximum(a, b)


def _mn(a, b):
    return None if a is None or b is None else jnp.minimum(a, b)


def _bitonic_merge_desc(z):
    n = len(z)
    j = n // 2
    while j >= 1:
        for i in range(n):
            l = i ^ j
            if l > i:
                z[i], z[l] = _mx(z[i], z[l]), _mn(z[i], z[l])
        j //= 2
    return z


def _sort_desc(v):
    n = len(v)
    v = list(v)
    k = 2
    while k <= n:
        j = k // 2
        while j >= 1:
            for i in range(n):
                l = i ^ j
                if l > i:
                    hi, lo = jnp.maximum(v[i], v[l]), jnp.minimum(v[i], v[l])
                    v[i], v[l] = (hi, lo) if (i & k) == 0 else (lo, hi)
            j //= 2
        k *= 2
    return v


def _merge_top_desc(x, y, n=PEER_TOPK):
    x = list(x) + [None] * (n - len(x))
    y = list(y) + [None] * (n - len(y))
    return _bitonic_merge_desc([_mx(x[i], y[n - 1 - i]) for i in range(n)])


def _pack_bf16_pair(lo, hi):
    lo_u = pltpu.bitcast(lo.astype(BF16).astype(F32), jnp.uint32)
    hi_u = pltpu.bitcast(hi.astype(BF16).astype(F32), jnp.uint32)
    return (lo_u >> 16) | (hi_u & jnp.uint32(0xFFFF0000))


def _route_head(s, roll):
    tm = s[0].shape[1]
    ngrp = tm // LANES
    nvr = N_KEYS // 8
    sub = lax.broadcasted_iota(jnp.int32, (8, LANES), 0)

    def top16_replicated(s_tile):
        v = _sort_desc([s_tile[8 * m:8 * m + 8, :] for m in range(nvr)])
        for shift in (4, 2, 1):
            v = _merge_top_desc(v, [roll(a, shift) for a in v])
        return v

    def dense(reps):
        out = reps[0]
        for g in range(1, ngrp):
            out = jnp.where(sub == g, reps[g], out)
        return out

    tops = [[top16_replicated(s[half][:, g * LANES:(g + 1) * LANES]) for g in range(ngrp)] for half in range(2)]
    t1 = [dense([tops[0][g][i] for g in range(ngrp)]) for i in range(PEER_TOPK)]
    t2 = [dense([tops[1][g][j] for g in range(ngrp)]) for j in range(PEER_TOPK)]
    rows = [[t1[i] + t2[j] for j in range(PEER_TOPK // (i + 1))] for i in range(PEER_TOPK)]
    singles = [rows[i][0] for i in range(PEER_TOPK // 2, PEER_TOPK)]
    m01 = _merge_top_desc(rows[0], rows[1])
    m23 = _merge_top_desc(rows[2], rows[3])
    m45 = _merge_top_desc(rows[4], rows[5])
    m67 = _merge_top_desc(rows[6], rows[7])
    z = _merge_top_desc(_merge_top_desc(_merge_top_desc(m01, m23), _merge_top_desc(m45, m67)), singles)
    tau = z[PEER_TOPK - 1]
    zsum = jnp.exp(z[0] - z[0])
    for kk in range(1, PEER_TOPK):
        zsum = zsum + jnp.exp(z[kk] - z[0])
    inv_z = 1.0 / zsum
    gt = [sum((cij > tau).astype(F32) for cij in rows[i]) for i in range(PEER_TOPK)]
    eq = [sum((cij == tau).astype(F32) for cij in rows[i]) for i in range(PEER_TOPK)]
    need = float(PEER_TOPK) - sum(gt)
    cnt = []
    for i in range(PEER_TOPK):
        cnt.append(gt[i] + jnp.clip(need, 0.0, eq[i]))
        need = need - eq[i]
    step = [cnt[i] - (cnt[i + 1] if i + 1 < PEER_TOPK else 0.0) for i in range(PEER_TOPK)]
    n_out, e1_out, r2_out, e2_out = [], [], [], []
    for g in range(ngrp):
        ls = slice(g * LANES, (g + 1) * LANES)
        rep = lambda dv: jnp.broadcast_to(dv[g:g + 1, :], (8, LANES))
        step_g = [rep(a) for a in step]
        inv_z_g = rep(inv_z)
        t1g, t2g = tops[0][g], tops[1][g]
        n_g, e1_g, rank2, gate2 = [], [], [], []
        for m in range(nvr):
            ks = slice(8 * m, 8 * m + 8)
            s1v = s[0][ks, ls]
            s2v = s[1][ks, ls]
            nv = jnp.where(t1g[0] <= s1v, step_g[0], 0.0)
            rv = (t2g[0] > s2v).astype(F32)
            for i in range(1, PEER_TOPK):
                nv = nv + jnp.where(t1g[i] <= s1v, step_g[i], 0.0)
                rv = rv + (t2g[i] > s2v).astype(F32)
            n_g.append(nv)
            e1_g.append(jnp.exp(s1v - t1g[0]))
            rank2.append(rv)
            gate2.append(jnp.exp(s2v - t2g[0]) * inv_z_g)
        n_out.append(n_g)
        e1_out.append(e1_g)
        r2_out.append([(rank2[m], rank2[m + nvr // 2]) for m in range(nvr // 2)])
        e2_out.append([(gate2[m], gate2[m + nvr // 2]) for m in range(nvr // 2)])
    return n_out, e1_out, r2_out, e2_out


def _route_body(x_ref, sc_ref, sh_ref, wq_ref, sk_ref, h_ref, r2_ref, e2_ref, n_ref, e1_ref, qt_ref):
    h = x_ref[...] * (1.0 + sc_ref[0]) + sh_ref[0]
    ht = _bf(h.T)
    h_ref[...] = ht
    qt_ref[...] = _dot(wq_ref[...], ht)

    def head(r, carry):
        r0 = pl.multiple_of(r * 2 * HALF_KEY, 2 * HALF_KEY)
        s = [_dot(sk_ref[half], _bf(qt_ref[pl.ds(r0 + half * HALF_KEY, HALF_KEY), :])) for half in range(2)]
        n_out, e1_out, r2_out, e2_out = _route_head(s, lambda a, shift: pltpu.roll(a, shift, 0))
        for g in range(len(n_out)):
            ls = slice(g * LANES, (g + 1) * LANES)
            for m in range(N_KEYS // 8):
                ks = slice(8 * m, 8 * m + 8)
                n_ref[r, ks, ls] = n_out[g][m]
                e1_ref[r, ks, ls] = e1_out[g][m]
            for m in range(N_KEYS // 16):
                ks = slice(8 * m, 8 * m + 8)
                r2_ref[r, ks, ls] = _pack_bf16_pair(*r2_out[g][m])
                e2_ref[r, ks, ls] = _pack_bf16_pair(*e2_out[g][m])
        return carry

    lax.fori_loop(0, R_HEADS, head, 0)


def _route(x, sc, sh, wq_t, sk, rows_per_cond):
    nt, d = x.shape
    tm = 1024
    cond = lambda i: (i * tm) // rows_per_cond
    f_spec = pl.BlockSpec((R_HEADS, N_KEYS, tm), lambda i: (0, 0, i))
    u_spec = pl.BlockSpec((R_HEADS, N_KEYS // 2, tm), lambda i: (0, 0, i))
    f_shape = jax.ShapeDtypeStruct((R_HEADS, N_KEYS, nt), F32)
    u_shape = jax.ShapeDtypeStruct((R_HEADS, N_KEYS // 2, nt), jnp.uint32)
    return pl.pallas_call(
        _route_body,
        grid=(nt // tm,),
        in_specs=[pl.BlockSpec((tm, d), lambda i: (i, 0)),
                  pl.BlockSpec((1, 1, d), lambda i: (cond(i), 0, 0)),
                  pl.BlockSpec((1, 1, d), lambda i: (cond(i), 0, 0)),
                  pl.BlockSpec(wq_t.shape, lambda i: (0, 0)),
                  pl.BlockSpec(sk.shape, lambda i: (0, 0, 0))],
        out_specs=[pl.BlockSpec((d, tm), lambda i: (0, i)), u_spec, u_spec, f_spec, f_spec],
        out_shape=[jax.ShapeDtypeStruct((d, nt), BF16), u_shape, u_shape, f_shape, f_shape],
        scratch_shapes=[pltpu.VMEM((wq_t.shape[0], tm), F32)],
        compiler_params=_params(("parallel",)),
        name="peer_route",
    )(x, sc, sh, wq_t, sk)


def _peer_tables(u, v):
    ne, d = u.shape
    perm = lambda t: t.reshape(ne // N_KEYS, 2, N_KEYS // 2, d).transpose(0, 2, 1, 3).reshape(ne, d)
    return _bf(perm(u)), _bf(perm(v).T)


def _gelu_tanh(x):
    return 0.5 * x * (1.0 + jnp.tanh(math.sqrt(2.0 / math.pi) * (x + 0.044715 * (x * x * x))))


def _expert_body(alpha, ht_ref, r2_ref, e2_ref, n_ref, e1_ref, u_ref, vt_ref, x_ref, g2_ref,
                 lng_ref, lnb_ref, o_ref, acc_ref, gt_ref):
    e = pl.program_id(1)

    @pl.when(e == 0)
    def _():
        acc_ref[...] = jnp.zeros_like(acc_ref)

    te, tm = gt_ref.shape
    pk = 16
    at = _dot(u_ref[...], ht_ref[...])
    n_slab = te // N_KEYS
    a0 = pl.multiple_of(e * n_slab, n_slab)
    for al in range(n_slab):
        for pc in range(tm // LANES):
            ps = slice(pc * LANES, (pc + 1) * LANES)
            w = jnp.zeros((N_KEYS // pk, pk, LANES), BF16)
            for r in range(R_HEADS):
                n_row = jnp.broadcast_to(n_ref[r, pl.ds(a0, n_slab), ps][al:al + 1, :], (pk, LANES)).astype(BF16)
                e1_row = jnp.broadcast_to(e1_ref[r, pl.ds(a0, n_slab), ps][al:al + 1, :], (pk, LANES)).astype(BF16)
                rank2 = pltpu.bitcast(r2_ref[r, :, ps], BF16).reshape(N_KEYS // pk, pk, LANES)
                gate2 = pltpu.bitcast(e2_ref[r, :, ps], BF16).reshape(N_KEYS // pk, pk, LANES)
                w = w + jnp.where(rank2 < n_row[None], gate2, jnp.zeros_like(gate2)) * e1_row[None]
            act = _gelu_tanh(at[al * N_KEYS:(al + 1) * N_KEYS, ps])
            gt_ref[al * N_KEYS:(al + 1) * N_KEYS, ps] = w.reshape(N_KEYS, LANES) * _bf(act)
    acc_ref[...] += _dot(vt_ref[...], gt_ref[...])

    @pl.when(e == pl.num_programs(1) - 1)
    def _():
        y = alpha * x_ref[...] + g2_ref[0] * acc_ref[...].T
        o_ref[...] = _layernorm_rows(y, lng_ref[...], lnb_ref[...])


def _experts(alpha, ht, r2, e2, n_sel, e1, u_b, vt_b, x, g2, lng, lnb, rows_per_cond):
    nt, d = x.shape
    ne = u_b.shape[0]
    tm, te = 512, 1024
    cond = lambda i: (i * tm) // rows_per_cond
    f_spec = pl.BlockSpec((R_HEADS, N_KEYS, tm), lambda i, e: (0, 0, i))
    u_spec = pl.BlockSpec((R_HEADS, N_KEYS // 2, tm), lambda i, e: (0, 0, i))
    return pl.pallas_call(
        functools.partial(_expert_body, alpha),
        grid=(nt // tm, ne // te),
        in_specs=[pl.BlockSpec((d, tm), lambda i, e: (0, i)),
                  u_spec, u_spec, f_spec, f_spec,
                  pl.BlockSpec((te, d), lambda i, e: (e, 0)),
                  pl.BlockSpec((d, te), lambda i, e: (0, e)),
                  pl.BlockSpec((tm, d), lambda i, e: (i, 0)),
                  pl.BlockSpec((1, 1, d), lambda i, e: (cond(i), 0, 0)),
                  pl.BlockSpec(lng.shape, lambda i, e: (0, 0)),
                  pl.BlockSpec(lnb.shape, lambda i, e: (0, 0))],
        out_specs=pl.BlockSpec((tm, d), lambda i, e: (i, 0)),
        out_shape=jax.ShapeDtypeStruct((nt, d), F32),
        scratch_shapes=[pltpu.VMEM((d, tm), F32), pltpu.VMEM((te, tm), BF16)],
        compiler_params=_params(("parallel", "arbitrary")),
        name="peer_experts",
    )(ht, r2, e2, n_sel, e1, u_b, vt_b, x, g2, lng, lnb)


def _l2norm(x):
    return x * lax.rsqrt(jnp.sum(jnp.square(x), -1, keepdims=True) + NORM_EPS)


def _grid_rotary(n_tok):
    rows = n_tok // GRID_W
    r = jnp.repeat(jnp.arange(rows), GRID_W).astype(F32)
    col = jnp.tile(jnp.arange(GRID_W), rows).astype(F32)
    n_freq = HEAD_DIM // 4
    inv = ROPE_BASE ** (-jnp.arange(n_freq, dtype=F32) / n_freq)
    ang = jnp.concatenate([r[:, None] * inv, col[:, None] * inv], -1)
    return jnp.cos(ang), jnp.sin(ang)


def _rope(x, cos, sin):
    x1, x2 = jnp.split(x, 2, -1)
    c, s = cos[None, :, None], sin[None, :, None]
    return jnp.concatenate([x1 * c - x2 * s, x1 * s + x2 * c], -1)


def _prep_group(proj, nb, t, conv_w, a_log, dt_bias, lb, rot):
    nt = nb * t
    heads = lambda a: a.reshape(nb, t, N_HEADS, HEAD_DIM)
    flat = lambda a: a.reshape(nt, MIX_W)
    x = proj[:, C_DNQKV:C_DNQKV + 3 * MIX_W].reshape(nb, t, 3 * MIX_W)
    xp = jnp.pad(x, ((0, 0), (CONV_K // 2, CONV_K // 2), (0, 0)))
    y = sum(xp[:, i:i + t, :] * conv_w[i][None, None, :] for i in range(CONV_K))
    y = jax.nn.silu(y)
    q, k, v = (heads(a) for a in jnp.split(y, 3, -1))
    q = _l2norm(q) * HEAD_DIM ** -0.5
    k = _l2norm(k)
    small = proj[:, C_SMALL:C_SMALL + 16]
    beta = jax.nn.sigmoid(small[:, 0:8]).reshape(nt, 2, N_HEADS)
    g = -jnp.exp(a_log) * jax.nn.softplus(small[:, 8:16].reshape(nt, 2, N_HEADS) + dt_bias)
    gb = jnp.concatenate([g, beta, jnp.zeros((nt, 2, LANES - 2 * N_HEADS), F32)], axis=-1)
    dn = (flat(q), flat(k), flat(v), jnp.swapaxes(gb, 0, 1))
    q, k, v = (heads(a) for a in jnp.split(proj[:, C_RTQKV:C_RTQKV + 3 * MIX_W], 3, -1))
    if rot is not None:
        q, k = _rope(q, *rot), _rope(k, *rot)
    rt = (flat(q * HEAD_DIM ** -0.5), flat(k), flat(v))
    q = jax.nn.silu(proj[:, C_HGQ:C_HGQ + MIX_W]) * HEAD_DIM ** -0.5
    f = lb + (1.0 - lb) * jax.nn.sigmoid(proj[:, C_HGF:C_HGF + 2 * MIX_W].reshape(nt, 2, MIX_W))
    f = jnp.swapaxes(f, 0, 1)
    hg = (q, 1.0 - f, jnp.log(f), proj[:, C_HGI:C_HGI + MIX_W])
    return dn, rt, hg


def _mix_group(proj, nb, t, conv_w, a_log, dt_bias, lg, lb, rot, s_dn, s_rt, s_hg):
    dn, rt, hg = _prep_group(proj, nb, t, conv_w, a_log, dt_bias, lb, rot)
    *o_dn, f_dn = _scan_call(_dn_body, "scan_delta",
                            [(dn[0], False), (dn[1], False), (dn[2], False), (dn[3], True)], s_dn, nb, t)
    *o_rt, f_rt = _scan_call(_rt_body, "scan_ret",
                            [(rt[0], False), (rt[1], False), (rt[2], False)], s_rt, nb, t, smem_inputs=(lg,))
    *o_hg, f_hg = _scan_call(_hg_body, "scan_hgrn",
                            [(hg[0], False), (hg[1], True), (hg[2], True), (hg[3], False)],
                            jnp.swapaxes(s_hg, -1, -2), nb, t)
    return (o_dn, o_rt, o_hg), (f_dn, f_rt, jnp.swapaxes(f_hg, -1, -2))


def kernel(x_prompt, x_sample, state_delta, state_ret, state_hgrn, c, c_ctx, ada_w, ada_b, w_in, dn_conv_w,
           dn_a_log, dn_dt_bias, rt_log_decay, hg_lb_logits, branch_norm_w, w_branch, w_out, ln_g, ln_b,
           peer_wq, peer_subkeys, peer_u, peer_v):
    nb1, t1, d = x_prompt.shape
    nb2, t2, _ = x_sample.shape
    depth = w_in.shape[0]
    alpha = (2 * depth) ** 0.25
    n1 = nb1 * t1
    assert n1 == t2, "modulation rows are looked up per block of t2 tokens"
    rows_per_cond = t2

    p_lb = jax.nn.softmax(hg_lb_logits.astype(F32), axis=0)
    lower_bounds = jnp.cumsum(p_lb, axis=0) - p_lb[0]
    rot = _grid_rotary(t2)

    cond8 = jnp.zeros((8, d), F32).at[0].set(c_ctx).at[1:1 + nb2].set(c)
    mod = _ada(cond8, ada_w, ada_b)

    x = jnp.concatenate([x_prompt.reshape(n1, d), x_sample.reshape(nb2 * t2, d)], axis=0)
    zero_state = jnp.zeros((nb1, 2, N_HEADS, HEAD_DIM, HEAD_DIM), F32)
    finals = []
    for l in range(depth):
        sh1, sc1, g1, sh2, sc2, g2 = (m.reshape(8, 1, d) for m in jnp.split(mod[l], 6, axis=-1))
        wl = w_in[l]
        seg = lambda a, b: wl[:, a:b]
        small = jnp.concatenate([seg(4 * MIX_W, 4 * MIX_W + 16), jnp.zeros((d, LANES - 16), F32)], axis=1)
        o0 = 4 * MIX_W + 16
        w_cat = _bf(jnp.concatenate([seg(o0 + 9 * MIX_W, o0 + 9 * MIX_W + 3 * d),
                                     seg(0, 4 * MIX_W),
                                     seg(o0, o0 + 9 * MIX_W),
                                     small], axis=1))
        proj = _inproj(x, sc1, sh1, w_cat, rows_per_cond)
        lg = -jnp.exp(rt_log_decay[l].astype(F32))
        common = (dn_conv_w[l], dn_a_log[l], dn_dt_bias[l], lg, lower_bounds[l])
        o1, f1 = _mix_group(proj[:n1], nb1, t1, *common, None, zero_state, zero_state, zero_state)
        o2, _ = _mix_group(proj[n1:], nb2, t2, *common, rot, state_delta[:, l], state_ret[:, l], state_hgrn[:, l])
        finals.append(f1)
        o_dn, o_rt, o_hg = ([jnp.concatenate([a, b], axis=0) for a, b in zip(m1, m2)] for m1, m2 in zip(o1, o2))
        x = _merge(alpha, o_dn, o_rt, o_hg, proj, x, g1, branch_norm_w[l], _bf(w_branch[l]), _bf(w_out[l]),
                   ln_g[l, 0:1], ln_b[l, 0:1], rows_per_cond)
        ht, r2, e2, n_sel, e1 = _route(x, sc2, sh2, _bf(peer_wq[l].T), _bf(peer_subkeys[l]), rows_per_cond)
        u_b, vt_b = _peer_tables(peer_u[l], peer_v[l])
        x = _experts(alpha, ht, r2, e2, n_sel, e1, u_b, vt_b, x, g2, ln_g[l, 1:2], ln_b[l, 1:2], rows_per_cond)

    y_prompt = x[:n1].reshape(nb1, t1, d)
    y_sample = x[n1:].reshape(nb2, t2, d)
    new_states = tuple(jnp.stack([f[i] for f in finals], axis=1) for i in range(3))
    return (y_prompt, y_sample) + new_states
```

```python
import functools
import math

import jax
import jax.numpy as jnp
from jax import lax
from jax.experimental import pallas as pl
from jax.experimental.pallas import tpu as pltpu

F32 = jnp.float32
BF16 = jnp.bfloat16

D_MODEL = 1024
N_HEADS = 4
HEAD_DIM = 128
MIX_W = N_HEADS * HEAD_DIM
N_DIR = 2
CHUNK = 64
SUB = 16
CONV_K = 5
GRID_W = 64
ROPE_BASE = 10000.0
N_KEYS = 128
R_HEADS = 8
HALF_KEY = 64
PEER_TOPK = 16
LN_EPS = 1e-5
NORM_EPS = 1e-6
LANES = 128
VMEM_LIMIT = 56 * 1024 * 1024

C_MERGE = 0
C_DNQKV = 3072
C_DNZ = 4608
C_RTQKV = 5120
C_RTG = 6656
C_HGQ = 7168
C_HGF = 7680
C_HGI = 8704
C_HGG = 9216
C_SMALL = 9728
N_COLS = 9856


def _dot(a, b):
    return jnp.dot(a, b, preferred_element_type=F32)


def _dot_nt(a, b):
    return lax.dot_general(a, b, (((1,), (1,)), ((), ())), preferred_element_type=F32)


def _dot_tn(a, b):
    return lax.dot_general(a, b, (((0,), (0,)), ((), ())), preferred_element_type=F32)


def _bf(x):
    return x.astype(BF16)


def _params(sem, limit=VMEM_LIMIT):
    return pltpu.CompilerParams(dimension_semantics=sem, vmem_limit_bytes=limit)


def _ada_body(c_ref, w_ref, b_ref, o_ref):
    c = c_ref[...]
    s = c * jax.nn.sigmoid(c)
    o_ref[0] = _dot(_bf(s), _bf(w_ref[0])) + b_ref[0]


def _ada(cond8, ada_w, ada_b):
    depth, d, n = ada_w.shape
    tn = 1536
    return pl.pallas_call(
        _ada_body,
        grid=(depth, n // tn),
        in_specs=[pl.BlockSpec((8, d), lambda l, j: (0, 0)),
                  pl.BlockSpec((1, d, tn), lambda l, j: (l, 0, j)),
                  pl.BlockSpec((1, 1, tn), lambda l, j: (l, 0, j))],
        out_specs=pl.BlockSpec((1, 8, tn), lambda l, j: (l, 0, j)),
        out_shape=jax.ShapeDtypeStruct((depth, 8, n), F32),
        compiler_params=_params(("parallel", "parallel")),
        name="ada_mod",
    )(cond8, ada_w, ada_b.reshape(depth, 1, n))


def _inproj_body(x_ref, sc_ref, sh_ref, w_ref, o_ref):
    h = x_ref[...] * (1.0 + sc_ref[0]) + sh_ref[0]
    o_ref[...] = _dot(_bf(h), w_ref[...])


def _inproj(x, sc, sh, w, rows_per_cond):
    nt, d = x.shape
    ncol = w.shape[1]
    tm, tn = 1024, 896
    cond = lambda i: (i * tm) // rows_per_cond
    return pl.pallas_call(
        _inproj_body,
        grid=(nt // tm, ncol // tn),
        in_specs=[pl.BlockSpec((tm, d), lambda i, j: (i, 0)),
                  pl.BlockSpec((1, 1, d), lambda i, j: (cond(i), 0, 0)),
                  pl.BlockSpec((1, 1, d), lambda i, j: (cond(i), 0, 0)),
                  pl.BlockSpec((d, tn), lambda i, j: (0, j))],
        out_specs=pl.BlockSpec((tm, tn), lambda i, j: (i, j)),
        out_shape=jax.ShapeDtypeStruct((nt, ncol), F32),
        compiler_params=_params(("parallel", "arbitrary")),
        name="in_proj",
    )(x, sc, sh, w)


def _split2(a):
    hi = _bf(a)
    return hi, _bf(a - hi.astype(F32))


def _dot3(ah, al, bh, bl):
    return _dot(ah, bh) + (_dot(ah, bl) + _dot(al, bh))


def _cumsum_rows(mask_bf, x):
    h1 = _bf(x)
    r1 = x - h1.astype(F32)
    h2 = _bf(r1)
    h3 = _bf(r1 - h2.astype(F32))
    return _dot(mask_bf, h1) + (_dot(mask_bf, h2) + _dot(mask_bf, h3))


def _chunk_dist(d):
    ii = lax.broadcasted_iota(jnp.int32, (CHUNK, CHUNK), 0)
    jj = lax.broadcasted_iota(jnp.int32, (CHUNK, CHUNK), 1)
    return ii - jj if d == 0 else jj - ii


def _last_row(x, d):
    return x[CHUNK - 1:CHUNK, :] if d == 0 else x[0:1, :]


def _head(h):
    return slice(h * HEAD_DIM, (h + 1) * HEAD_DIM)


_CHAINS = [(d, h) for d in range(N_DIR) for h in range(N_HEADS)]


def _unit_tri_solve(low, rhs):
    ii = lax.broadcasted_iota(jnp.int32, (CHUNK, CHUNK), 0)
    jj = lax.broadcasted_iota(jnp.int32, (CHUNK, CHUNK), 1)
    same_blk = (ii // SUB) == (jj // SUB)
    eye = (ii == jj).astype(F32)
    n_blk = CHUNK // SUB
    low_s = {n: _split2(low[n]) for n in _CHAINS}
    diag = {n: jnp.where(same_blk, low[n], 0.0) for n in _CHAINS}
    inv = {n: eye - diag[n] for n in _CHAINS}
    p = {n: _split2(diag[n]) for n in _CHAINS}
    for _ in range(int(math.log2(SUB)) - 1):
        p = {n: _split2(_dot3(*p[n], *p[n])) for n in _CHAINS}
        inv = {n: inv[n] + _dot3(*_split2(inv[n]), *p[n]) for n in _CHAINS}
    inv_s = {n: _split2(inv[n]) for n in _CHAINS}
    zero = jnp.zeros((SUB, rhs[_CHAINS[0]].shape[1]), F32)
    xb = {n: [zero] * n_blk for n in _CHAINS}
    for step in range(n_blk):
        rows = {n: slice((step if n[0] == 0 else n_blk - 1 - step) * SUB,
                         (step if n[0] == 0 else n_blk - 1 - step) * SUB + SUB) for n in _CHAINS}
        if step == 0:
            resid = rhs
        else:
            done = {n: _split2(jnp.concatenate(xb[n], axis=0)) for n in _CHAINS}
            resid = {n: rhs[n] - _dot3(*low_s[n], *done[n]) for n in _CHAINS}
        sol = {n: _dot3(inv_s[n][0][rows[n]], inv_s[n][1][rows[n]], *_split2(resid[n])) for n in _CHAINS}
        for n in _CHAINS:
            blk = step if n[0] == 0 else n_blk - 1 - step
            xb[n] = xb[n][:blk] + [sol[n]] + xb[n][blk + 1:]
    return {n: jnp.concatenate(xb[n], axis=0) for n in _CHAINS}


def _dn_body(qf_ref, kf_ref, vf_ref, gf_ref, qb_ref, kb_ref, vb_ref, gb_ref, s0_ref, of_ref, ob_ref, sf_ref, s_ref):
    c = pl.program_id(1)

    @pl.when(c == 0)
    def _():
        s_ref[...] = s0_ref[0]

    refs = ((qf_ref, kf_ref, vf_ref, gf_ref, of_ref), (qb_ref, kb_ref, vb_ref, gb_ref, ob_ref))
    incl, strict, gbs, gcs, gcts = [], [], [], [], []
    for d in range(N_DIR):
        dist = _chunk_dist(d)
        incl.append(dist >= 0)
        strict.append(dist > 0)
        g = refs[d][3][...]
        gc = _cumsum_rows(incl[d].astype(BF16), g)
        gbs.append(g)
        gcs.append(gc)
        gcts.append(gc.T)
    q, k, v, g_col, beta, dmask = {}, {}, {}, {}, {}, {}
    gcol = lambda n: N_DIR * N_HEADS + n[0] * N_HEADS + n[1]
    for n in _CHAINS:
        d, h = n
        q[n] = refs[d][0][:, _head(h)]
        k[n] = refs[d][1][:, _head(h)]
        v[n] = refs[d][2][:, _head(h)]
        g_col[n] = gcs[d][:, gcol(n):gcol(n) + 1]
        beta[n] = gbs[d][:, d * N_HEADS + h:d * N_HEADS + h + 1]
        diff = g_col[n] - gcts[d][gcol(n):gcol(n) + 1, :]
        dmask[n] = jnp.where(incl[d], jnp.exp(jnp.where(incl[d], diff, 0.0)), 0.0)
    kb = {n: k[n] * beta[n] for n in _CHAINS}
    k_b = {n: _bf(k[n]) for n in _CHAINS}
    kk = {n: _dot_nt(_bf(kb[n]), k_b[n]) for n in _CHAINS}
    qk = {n: _dot_nt(_bf(q[n]), k_b[n]) for n in _CHAINS}
    low = {n: jnp.where(strict[n[0]], kk[n] * dmask[n], 0.0) for n in _CHAINS}
    rhs = {n: jnp.concatenate([v[n] * beta[n], kb[n] * jnp.exp(g_col[n])], axis=1) for n in _CHAINS}
    x = _unit_tri_solve(low, rhs)
    s = {n: s_ref[n[0], n[1]] for n in _CHAINS}
    s_b = {n: _bf(s[n]) for n in _CHAINS}
    ws = {n: _dot(_bf(x[n][:, HEAD_DIM:]), s_b[n]) for n in _CHAINS}
    qs = {n: _dot(_bf(q[n] * jnp.exp(g_col[n])), s_b[n]) for n in _CHAINS}
    v_new = {n: _bf(x[n][:, :HEAD_DIM] - ws[n]) for n in _CHAINS}
    av = {n: _dot(_bf(qk[n] * dmask[n]), v_new[n]) for n in _CHAINS}
    g_last = {n: _last_row(gcs[n[0]], n[0])[:, gcol(n):gcol(n) + 1] for n in _CHAINS}
    kv = {n: _dot_tn(_bf(k[n] * jnp.exp(g_last[n] - g_col[n])), v_new[n]) for n in _CHAINS}
    for n in _CHAINS:
        d, h = n
        s_ref[d, h] = s[n] * jnp.exp(g_last[n]) + kv[n]
        refs[d][4][:, _head(h)] = qs[n] + av[n]

    @pl.when(c == pl.num_programs(1) - 1)
    def _():
        sf_ref[0] = s_ref[...]


def _rt_body(lg_ref, qf_ref, kf_ref, vf_ref, qb_ref, kb_ref, vb_ref, s0_ref, of_ref, ob_ref, sf_ref, s_ref):
    c = pl.program_id(1)

    @pl.when(c == 0)
    def _():
        s_ref[...] = s0_ref[0]

    refs = ((qf_ref, kf_ref, vf_ref, of_ref), (qb_ref, kb_ref, vb_ref, ob_ref))
    pcol = lax.broadcasted_iota(jnp.int32, (CHUNK, 1), 0)
    q, k, v, qk, decay, q_dec, k_dec, c_dec = {}, {}, {}, {}, {}, {}, {}, {}
    for n in _CHAINS:
        d, h = n
        dist = _chunk_dist(d)
        pos = (pcol if d == 0 else CHUNK - 1 - pcol).astype(F32)
        lg = lg_ref[d, h]
        q[n] = refs[d][0][:, _head(h)]
        k[n] = refs[d][1][:, _head(h)]
        v[n] = _bf(refs[d][2][:, _head(h)])
        decay[n] = jnp.where(dist >= 0, jnp.exp(jnp.maximum(dist, 0).astype(F32) * lg), 0.0)
        q_dec[n] = jnp.exp((pos + 1.0) * lg)
        k_dec[n] = jnp.exp((CHUNK - 1.0 - pos) * lg)
        c_dec[n] = jnp.exp(jnp.full((1, 1), CHUNK, F32) * lg)
    qk = {n: _dot_nt(_bf(q[n]), _bf(k[n])) for n in _CHAINS}
    s = {n: s_ref[n[0], n[1]] for n in _CHAINS}
    qs = {n: _dot(_bf(q[n] * q_dec[n]), _bf(s[n])) for n in _CHAINS}
    av = {n: _dot(_bf(qk[n] * decay[n]), v[n]) for n in _CHAINS}
    kv = {n: _dot_tn(_bf(k[n] * k_dec[n]), v[n]) for n in _CHAINS}
    for n in _CHAINS:
        d, h = n
        s_ref[d, h] = s[n] * c_dec[n] + kv[n]
        refs[d][3][:, _head(h)] = av[n] + qs[n]

    @pl.when(c == pl.num_programs(1) - 1)
    def _():
        sf_ref[0] = s_ref[...]


def _hg_body(qf_ref, kf_ref, lf_ref, vf_ref, qb_ref, kb_ref, lb_ref, vb_ref, s0_ref, of_ref, ob_ref, sf_ref,
             s_ref):
    c = pl.program_id(1)

    @pl.when(c == 0)
    def _():
        s_ref[...] = s0_ref[0]

    refs = ((qf_ref, kf_ref, lf_ref, vf_ref, of_ref), (qb_ref, kb_ref, lb_ref, vb_ref, ob_ref))
    rloc = lax.broadcasted_iota(jnp.int32, (SUB, 1), 0)
    jcol = lax.broadcasted_iota(jnp.int32, (SUB, CHUNK), 1)
    incl_bf = [(_chunk_dist(d) >= 0).astype(BF16) for d in range(N_DIR)]
    q, k, v, gc = {}, {}, {}, {}
    for n in _CHAINS:
        d, h = n
        q[n] = refs[d][0][:, _head(h)]
        k[n] = refs[d][1][0, :, _head(h)]
        v[n] = refs[d][3][:, _head(h)]
    for n in _CHAINS:
        gc[n] = _cumsum_rows(incl_bf[n[0]], refs[n[0]][2][0, :, _head(n[1])])
    st = {n: s_ref[n[0], n[1]] for n in _CHAINS}
    v_b = {n: _bf(v[n]) for n in _CHAINS}
    o_inter = {n: _dot_nt(_bf(q[n] * jnp.exp(gc[n])), _bf(st[n])) for n in _CHAINS}
    g_last = {n: _last_row(gc[n], n[0]) for n in _CHAINS}
    kv = {n: _dot_tn(v_b[n], _bf(k[n] * jnp.exp(g_last[n] - gc[n]))) for n in _CHAINS}
    for blk in range(CHUNK // SUB):
        r0 = blk * SUB
        rs = slice(r0, r0 + SUB)
        a = {}
        for n in _CHAINS:
            d = n[0]
            ref = gc[n][r0:r0 + 1, :] if d == 0 else gc[n][r0 + SUB - 1:r0 + SUB, :]
            q_t = q[n][rs] * jnp.exp(gc[n][rs] - ref)
            k_t = k[n] * jnp.exp(jnp.minimum(ref - gc[n], 0.0))
            earlier = (jcol < r0) if d == 0 else (jcol >= r0 + SUB)
            a[n] = _bf(jnp.where(earlier, _dot_nt(_bf(q_t), _bf(k_t)), 0.0))
        o_off = {n: _dot(a[n], v_b[n]) for n in _CHAINS}
        for n in _CHAINS:
            d, h = n
            o_blk = o_off[n] + o_inter[n][rs]
            g_blk = gc[n][rs]
            q_blk = q[n][rs]
            for jl in range(SUB):
                j = r0 + jl
                pj = q_blk * k[n][j:j + 1, :] * jnp.exp(jnp.minimum(g_blk - gc[n][j:j + 1, :], 0.0))
                aj = jnp.sum(pj, axis=1, keepdims=True)
                keep = (rloc >= jl) if d == 0 else (rloc <= jl)
                o_blk = o_blk + jnp.where(keep, aj, 0.0) * v[n][j:j + 1, :]
            refs[d][4][rs, _head(h)] = o_blk
    for n in _CHAINS:
        s_ref[n[0], n[1]] = st[n] * jnp.exp(g_last[n]) + kv[n]

    @pl.when(c == pl.num_programs(1) - 1)
    def _():
        sf_ref[0] = s_ref[...]


def _scan_call(body, name, seq_inputs, s0, nb, t, row0, nt, o_prev=None, smem_inputs=()):
    nc = t // CHUNK
    c0 = row0 // CHUNK
    fwd = lambda b, c: c0 + b * nc + c
    bwd = lambda b, c: c0 + b * nc + nc - 1 - c
    in_specs = [pl.BlockSpec(memory_space=pltpu.SMEM) for _ in smem_inputs]
    args = list(smem_inputs)
    for d, blk in enumerate((fwd, bwd)):
        for arr, kind in seq_inputs:
            if kind == "dir":
                in_specs.append(pl.BlockSpec((1, CHUNK, arr.shape[-1]), lambda b, c, d=d, blk=blk: (d, blk(b, c), 0)))
            elif kind == "rows":
                in_specs.append(pl.BlockSpec((CHUNK, arr.shape[-1]), lambda b, c, blk=blk: (blk(b, c), 0)))
            else:
                in_specs.append(pl.BlockSpec((CHUNK, MIX_W), lambda b, c, blk=blk, col=kind: (blk(b, c), col)))
            args.append(arr)
    st_spec = pl.BlockSpec((1, N_DIR, N_HEADS, HEAD_DIM, HEAD_DIM), lambda b, c: (b, 0, 0, 0, 0))
    in_specs.append(st_spec)
    args.append(s0)
    n_in = len(args)
    aliases = {}
    if o_prev is not None:
        in_specs += [pl.BlockSpec(memory_space=pl.ANY)] * 2
        args += list(o_prev)
        aliases = {n_in: 0, n_in + 1: 1}
        inner = body
        body = lambda *refs: inner(*refs[:n_in], *refs[n_in + 2:])
    o_shape = jax.ShapeDtypeStruct((nt, MIX_W), F32)
    return pl.pallas_call(
        body,
        grid=(nb, nc),
        in_specs=in_specs,
        out_specs=[pl.BlockSpec((CHUNK, MIX_W), lambda b, c: (fwd(b, c), 0)),
                   pl.BlockSpec((CHUNK, MIX_W), lambda b, c: (bwd(b, c), 0)), st_spec],
        out_shape=[o_shape, o_shape, jax.ShapeDtypeStruct((nb, N_DIR, N_HEADS, HEAD_DIM, HEAD_DIM), F32)],
        scratch_shapes=[pltpu.VMEM((N_DIR, N_HEADS, HEAD_DIM, HEAD_DIM), F32)],
        input_output_aliases=aliases,
        compiler_params=_params(("parallel", "arbitrary")),
        name=name,
    )(*args)


def _layernorm_rows(y, g, b):
    mu = jnp.mean(y, axis=-1, keepdims=True)
    yc = y - mu
    var = jnp.mean(yc * yc, axis=-1, keepdims=True)
    return yc * lax.rsqrt(var + LN_EPS) * g + b


def _silu(x):
    return x * jax.nn.sigmoid(x)


def _merge_body(alpha, odn_f, odn_b, ort_f, ort_b, ohg_f, ohg_b, z_ref, rg_ref, hgg_ref, mg_ref, x_ref, g1_ref,
                bnw_ref, wb_ref, wo_ref, lng_ref, lnb_ref, o_ref):
    bnw = bnw_ref[...]

    def branch(o_pair, gate_ref, n, centered):
        o = o_pair[0][...] + o_pair[1][...]
        parts = []
        for h in range(N_HEADS):
            hs = slice(h * HEAD_DIM, (h + 1) * HEAD_DIM)
            xh = o[:, hs]
            if centered:
                xh = xh - jnp.mean(xh, axis=-1, keepdims=True)
            ms = jnp.mean(xh * xh, axis=-1, keepdims=True)
            parts.append(xh * lax.rsqrt(ms + NORM_EPS) * bnw[n:n + 1, :])
        y = jnp.concatenate(parts, axis=1) * _silu(gate_ref[...])
        return _dot(_bf(y), wb_ref[n])

    mixed = None
    for n, (oref, gref, centered) in enumerate((((odn_f, odn_b), z_ref, False), ((ort_f, ort_b), rg_ref, True),
                                                ((ohg_f, ohg_b), hgg_ref, False))):
        pb = branch(oref, gref, n, centered)
        gate = jax.nn.sigmoid(mg_ref[:, n * D_MODEL:(n + 1) * D_MODEL])
        mixed = gate * pb if mixed is None else mixed + gate * pb
    mix = _dot(_bf(mixed), wo_ref[...])
    y = alpha * x_ref[...] + g1_ref[0] * mix
    o_ref[...] = _layernorm_rows(y, lng_ref[...], lnb_ref[...])


def _merge(alpha, o_dn, o_rt, o_hg, proj, x, g1, bnw, wb, wo, lng, lnb, rows_per_cond):
    nt, d = x.shape
    tm = 256
    cond = lambda i: (i * tm) // rows_per_cond
    o_spec = pl.BlockSpec((tm, MIX_W), lambda i: (i, 0))
    col = lambda c0: pl.BlockSpec((tm, MIX_W), lambda i: (i, c0 // MIX_W))
    full = lambda a: pl.BlockSpec(a.shape, lambda i: (0,) * a.ndim)
    return pl.pallas_call(
        functools.partial(_merge_body, alpha),
        grid=(nt // tm,),
        in_specs=[o_spec] * 6 + [col(C_DNZ), col(C_RTG), col(C_HGG),
                  pl.BlockSpec((tm, 3 * D_MODEL), lambda i: (i, 0)),
                  pl.BlockSpec((tm, d), lambda i: (i, 0)),
                  pl.BlockSpec((1, 1, d), lambda i: (cond(i), 0, 0)),
                  full(bnw), full(wb), full(wo), full(lng), full(lnb)],
        out_specs=pl.BlockSpec((tm, d), lambda i: (i, 0)),
        out_shape=jax.ShapeDtypeStruct((nt, d), F32),
        compiler_params=_params(("parallel",)),
        name="merge_ln",
    )(*o_dn, *o_rt, *o_hg, proj, proj, proj, proj, x, g1, bnw, wb, wo, lng, lnb)


def _mx(a, b):
    if a is None:
        return b
    return a if b is None else jnp.maximum(a, b)


def _mn(a, b):
    return None if a is None or b is None else jnp.minimum(a, b)


def _bitonic_merge_desc(z):
    n = len(z)
    j = n // 2
    while j >= 1:
        for i in range(n):
            l = i ^ j
            if l > i:
                z[i], z[l] = _mx(z[i], z[l]), _mn(z[i], z[l])
        j //= 2
    return z


def _sort_desc(v):
    n = len(v)
    v = list(v)
    k = 2
    while k <= n:
        j = k // 2
        while j >= 1:
            for i in range(n):
                l = i ^ j
                if l > i:
                    hi, lo = jnp.maximum(v[i], v[l]), jnp.minimum(v[i], v[l])
                    v[i], v[l] = (hi, lo) if (i & k) == 0 else (lo, hi)
            j //= 2
        k *= 2
    return v


def _merge_top_desc(x, y, n=PEER_TOPK):
    x = list(x) + [None] * (n - len(x))
    y = list(y) + [None] * (n - len(y))
    return _bitonic_merge_desc([_mx(x[i], y[n - 1 - i]) for i in range(n)])


def _pack_bf16_pair(lo, hi):
    lo_u = pltpu.bitcast(lo.astype(BF16).astype(F32), jnp.uint32)
    hi_u = pltpu.bitcast(hi.astype(BF16).astype(F32), jnp.uint32)
    return (lo_u >> 16) | (hi_u & jnp.uint32(0xFFFF0000))


def _route_head(s, roll):
    tm = s[0].shape[1]
    ngrp = tm // LANES
    nvr = N_KEYS // 8
    sub = lax.broadcasted_iota(jnp.int32, (8, LANES), 0)

    def top16_replicated(s_tile):
        v = _sort_desc([s_tile[8 * m:8 * m + 8, :] for m in range(nvr)])
        for shift in (4, 2, 1):
            v = _merge_top_desc(v, [roll(a, shift) for a in v])
        return v

    def dense(reps):
        out = reps[0]
        for g in range(1, ngrp):
            out = jnp.where(sub == g, reps[g], out)
        return out

    tops = [[top16_replicated(s[half][:, g * LANES:(g + 1) * LANES]) for g in range(ngrp)] for half in range(2)]
    t1 = [dense([tops[0][g][i] for g in range(ngrp)]) for i in range(PEER_TOPK)]
    t2 = [dense([tops[1][g][j] for g in range(ngrp)]) for j in range(PEER_TOPK)]
    rows = [[t1[i] + t2[j] for j in range(PEER_TOPK // (i + 1))] for i in range(PEER_TOPK)]
    singles = [rows[i][0] for i in range(PEER_TOPK // 2, PEER_TOPK)]
    m01 = _merge_top_desc(rows[0], rows[1])
    m23 = _merge_top_desc(rows[2], rows[3])
    m45 = _merge_top_desc(rows[4], rows[5])
    m67 = _merge_top_desc(rows[6], rows[7])
    z = _merge_top_desc(_merge_top_desc(_merge_top_desc(m01, m23), _merge_top_desc(m45, m67)), singles)
    tau = z[PEER_TOPK - 1]
    zsum = jnp.exp(z[0] - z[0])
    for kk in range(1, PEER_TOPK):
        zsum = zsum + jnp.exp(z[kk] - z[0])
    inv_z = 1.0 / zsum
    gt = [sum((cij > tau).astype(F32) for cij in rows[i]) for i in range(PEER_TOPK)]
    eq = [sum((cij == tau).astype(F32) for cij in rows[i]) for i in range(PEER_TOPK)]
    need = float(PEER_TOPK) - sum(gt)
    cnt = []
    for i in range(PEER_TOPK):
        cnt.append(gt[i] + jnp.clip(need, 0.0, eq[i]))
        need = need - eq[i]
    step = [cnt[i] - (cnt[i + 1] if i + 1 < PEER_TOPK else 0.0) for i in range(PEER_TOPK)]
    n_out, e1_out, r2_out, e2_out = [], [], [], []
    for g in range(ngrp):
        ls = slice(g * LANES, (g + 1) * LANES)
        rep = lambda dv: jnp.broadcast_to(dv[g:g + 1, :], (8, LANES))
        step_g = [rep(a) for a in step]
        inv_z_g = rep(inv_z)
        t1g, t2g = tops[0][g], tops[1][g]
        n_g, e1_g, rank2, gate2 = [], [], [], []
        for m in range(nvr):
            ks = slice(8 * m, 8 * m + 8)
            s1v = s[0][ks, ls]
            s2v = s[1][ks, ls]
            nv = jnp.where(t1g[0] <= s1v, step_g[0], 0.0)
            rv = (t2g[0] > s2v).astype(F32)
            for i in range(1, PEER_TOPK):
                nv = nv + jnp.where(t1g[i] <= s1v, step_g[i], 0.0)
                rv = rv + (t2g[i] > s2v).astype(F32)
            n_g.append(nv)
            e1_g.append(jnp.exp(s1v - t1g[0]))
            rank2.append(rv)
            gate2.append(jnp.exp(s2v - t2g[0]) * inv_z_g)
        n_out.append(n_g)
        e1_out.append(e1_g)
        r2_out.append([(rank2[m], rank2[m + nvr // 2]) for m in range(nvr // 2)])
        e2_out.append([(gate2[m], gate2[m + nvr // 2]) for m in range(nvr // 2)])
    return n_out, e1_out, r2_out, e2_out


def _route_body(x_ref, sc_ref, sh_ref, wq_ref, sk_ref, h_ref, r2_ref, e2_ref, n_ref, e1_ref, qt_ref):
    h = x_ref[...] * (1.0 + sc_ref[0]) + sh_ref[0]
    ht = _bf(h.T)
    h_ref[...] = ht
    qt_ref[...] = _dot(wq_ref[...], ht)

    def head(r, carry):
        r0 = pl.multiple_of(r * 2 * HALF_KEY, 2 * HALF_KEY)
        s = [_dot(sk_ref[half], _bf(qt_ref[pl.ds(r0 + half * HALF_KEY, HALF_KEY), :])) for half in range(2)]
        n_out, e1_out, r2_out, e2_out = _route_head(s, lambda a, shift: pltpu.roll(a, shift, 0))
        for g in range(len(n_out)):
            ls = slice(g * LANES, (g + 1) * LANES)
            for m in range(N_KEYS // 8):
                ks = slice(8 * m, 8 * m + 8)
                n_ref[r, ks, ls] = n_out[g][m]
                e1_ref[r, ks, ls] = e1_out[g][m]
            for m in range(N_KEYS // 16):
                ks = slice(8 * m, 8 * m + 8)
                r2_ref[r, ks, ls] = _pack_bf16_pair(*r2_out[g][m])
                e2_ref[r, ks, ls] = _pack_bf16_pair(*e2_out[g][m])
        return carry

    lax.fori_loop(0, R_HEADS, head, 0)


def _route(x, sc, sh, wq_t, sk, rows_per_cond):
    nt, d = x.shape
    tm = 1024
    cond = lambda i: (i * tm) // rows_per_cond
    f_spec = pl.BlockSpec((R_HEADS, N_KEYS, tm), lambda i: (0, 0, i))
    u_spec = pl.BlockSpec((R_HEADS, N_KEYS // 2, tm), lambda i: (0, 0, i))
    f_shape = jax.ShapeDtypeStruct((R_HEADS, N_KEYS, nt), F32)
    u_shape = jax.ShapeDtypeStruct((R_HEADS, N_KEYS // 2, nt), jnp.uint32)
    return pl.pallas_call(
        _route_body,
        grid=(nt // tm,),
        in_specs=[pl.BlockSpec((tm, d), lambda i: (i, 0)),
                  pl.BlockSpec((1, 1, d), lambda i: (cond(i), 0, 0)),
                  pl.BlockSpec((1, 1, d), lambda i: (cond(i), 0, 0)),
                  pl.BlockSpec(wq_t.shape, lambda i: (0, 0)),
                  pl.BlockSpec(sk.shape, lambda i: (0, 0, 0))],
        out_specs=[pl.BlockSpec((d, tm), lambda i: (0, i)), u_spec, u_spec, f_spec, f_spec],
        out_shape=[jax.ShapeDtypeStruct((d, nt), BF16), u_shape, u_shape, f_shape, f_shape],
        scratch_shapes=[pltpu.VMEM((wq_t.shape[0], tm), F32)],
        compiler_params=_params(("parallel",)),
        name="peer_route",
    )(x, sc, sh, wq_t, sk)


def _interleave_subkeys(sk):
    half = N_KEYS // 2
    second = sk[1].reshape(half, 2, sk.shape[-1]).transpose(1, 0, 2).reshape(N_KEYS, sk.shape[-1])
    return jnp.stack([sk[0], second])


def _gelu_tanh(x):
    return 0.5 * x * (1.0 + jnp.tanh(math.sqrt(2.0 / math.pi) * (x + 0.044715 * (x * x * x))))


def _expert_body(alpha, ht_ref, r2_ref, e2_ref, n_ref, e1_ref, u_ref, vt_ref, x_ref, g2_ref,
                 lng_ref, lnb_ref, o_ref, acc_ref, gt_ref):
    e = pl.program_id(1)

    @pl.when(e == 0)
    def _():
        acc_ref[...] = jnp.zeros_like(acc_ref)

    te, tm = gt_ref.shape
    pk = 16
    at = _dot(u_ref[...], ht_ref[...])
    n_slab = te // N_KEYS
    a0 = pl.multiple_of(e * n_slab, n_slab)
    for al in range(n_slab):
        for pc in range(tm // LANES):
            ps = slice(pc * LANES, (pc + 1) * LANES)
            w = jnp.zeros((N_KEYS // pk, pk, LANES), BF16)
            for r in range(R_HEADS):
                n_row = jnp.broadcast_to(n_ref[r, pl.ds(a0, n_slab), ps][al:al + 1, :], (pk, LANES)).astype(BF16)
                e1_row = jnp.broadcast_to(e1_ref[r, pl.ds(a0, n_slab), ps][al:al + 1, :], (pk, LANES)).astype(BF16)
                rank2 = pltpu.bitcast(r2_ref[r, :, ps], BF16).reshape(N_KEYS // pk, pk, LANES)
                gate2 = pltpu.bitcast(e2_ref[r, :, ps], BF16).reshape(N_KEYS // pk, pk, LANES)
                w = w + jnp.where(rank2 < n_row[None], gate2, jnp.zeros_like(gate2)) * e1_row[None]
            act = _gelu_tanh(at[al * N_KEYS:(al + 1) * N_KEYS, ps])
            gt_ref[al * N_KEYS:(al + 1) * N_KEYS, ps] = w.reshape(N_KEYS, LANES) * _bf(act)
    acc_ref[...] += _dot(vt_ref[...], gt_ref[...])

    @pl.when(e == pl.num_programs(1) - 1)
    def _():
        y = alpha * x_ref[...] + g2_ref[0] * acc_ref[...].T
        o_ref[...] = _layernorm_rows(y, lng_ref[...], lnb_ref[...])


def _experts(alpha, ht, r2, e2, n_sel, e1, u_b, vt_b, x, g2, lng, lnb, rows_per_cond):
    nt, d = x.shape
    ne = u_b.shape[0]
    tm, te = 512, 1024
    cond = lambda i: (i * tm) // rows_per_cond
    f_spec = pl.BlockSpec((R_HEADS, N_KEYS, tm), lambda i, e: (0, 0, i))
    u_spec = pl.BlockSpec((R_HEADS, N_KEYS // 2, tm), lambda i, e: (0, 0, i))
    return pl.pallas_call(
        functools.partial(_expert_body, alpha),
        grid=(nt // tm, ne // te),
        in_specs=[pl.BlockSpec((d, tm), lambda i, e: (0, i)),
                  u_spec, u_spec, f_spec, f_spec,
                  pl.BlockSpec((te, d), lambda i, e: (e, 0)),
                  pl.BlockSpec((d, te), lambda i, e: (0, e)),
                  pl.BlockSpec((tm, d), lambda i, e: (i, 0)),
                  pl.BlockSpec((1, 1, d), lambda i, e: (cond(i), 0, 0)),
                  pl.BlockSpec(lng.shape, lambda i, e: (0, 0)),
                  pl.BlockSpec(lnb.shape, lambda i, e: (0, 0))],
        out_specs=pl.BlockSpec((tm, d), lambda i, e: (i, 0)),
        out_shape=jax.ShapeDtypeStruct((nt, d), F32),
        scratch_shapes=[pltpu.VMEM((d, tm), F32), pltpu.VMEM((te, tm), BF16)],
        compiler_params=_params(("parallel", "arbitrary")),
        name="peer_experts",
    )(ht, r2, e2, n_sel, e1, u_b, vt_b, x, g2, lng, lnb)


PREP_ROWS = 256
HALO_ROWS = 8


def _grid_rotary(n_tok):
    rows = n_tok // GRID_W
    r = jnp.repeat(jnp.arange(rows), GRID_W).astype(F32)
    col = jnp.tile(jnp.arange(GRID_W), rows).astype(F32)
    n_freq = HEAD_DIM // 4
    inv = ROPE_BASE ** (-jnp.arange(n_freq, dtype=F32) / n_freq)
    ang = jnp.concatenate([r[:, None] * inv, col[:, None] * inv], -1)
    return jnp.cos(ang), jnp.sin(ang)


def _prep_body(blocks_1, bps_1, bps_2, qkv_ref, prev_ref, next_ref, small_ref, rt_ref, hq_ref, hf0_ref, hf1_ref,
               cos_ref, sin_ref, cw_ref, avec_ref, bvec_ref, lb_ref,
               dq_ref, dk_ref, dv_ref, gb_ref, rq_ref, rk_ref, gq_ref, gk_ref, glf_ref):
    j = pl.program_id(0)
    pos = jnp.where(j < blocks_1, j % bps_1, (j - blocks_1) % bps_2)
    bps = jnp.where(j < blocks_1, bps_1, bps_2)
    scale = HEAD_DIM ** -0.5
    rows = qkv_ref.shape[0]
    prev = jnp.where(pos == 0, 0.0, prev_ref[...])
    nxt = jnp.where(pos == bps - 1, 0.0, next_ref[...])
    xcat = jnp.concatenate([prev, qkv_ref[...], nxt], axis=0)
    y = None
    for i in range(CONV_K):
        shift = (CONV_K // 2 - i) % xcat.shape[0]
        r = pltpu.roll(xcat, shift, 0) if shift else xcat
        term = r[HALO_ROWS:HALO_ROWS + rows] * cw_ref[i:i + 1, :]
        y = term if y is None else y + term
    y = _silu(y)
    for h in range(N_HEADS):
        for part, ref, mult in ((0, dq_ref, scale), (1, dk_ref, None)):
            xh = y[:, part * MIX_W + h * HEAD_DIM:part * MIX_W + (h + 1) * HEAD_DIM]
            xh = xh * lax.rsqrt(jnp.sum(xh * xh, axis=-1, keepdims=True) + NORM_EPS)
            ref[:, _head(h)] = xh if mult is None else xh * mult
    dv_ref[...] = y[:, 2 * MIX_W:]
    sm = small_ref[...]
    lane = lax.broadcasted_iota(jnp.int32, sm.shape, 1)
    z = sm + bvec_ref[...]
    softplus = jnp.maximum(z, 0.0) + jnp.log(1.0 + jnp.exp(-jnp.abs(z)))
    gb_ref[...] = jnp.where(lane < N_DIR * N_HEADS, jax.nn.sigmoid(sm),
                            jnp.where(lane < 2 * N_DIR * N_HEADS, avec_ref[...] * softplus, 0.0))
    for part, ref, mult in ((0, rq_ref, scale), (1, rk_ref, None)):
        for h in range(N_HEADS):
            xh = rt_ref[:, part * MIX_W + h * HEAD_DIM:part * MIX_W + (h + 1) * HEAD_DIM]
            xh = xh * cos_ref[...] + pltpu.roll(xh, HEAD_DIM // 2, 1) * sin_ref[...]
            ref[:, _head(h)] = xh if mult is None else xh * mult
    gq_ref[...] = _silu(hq_ref[...]) * scale
    lb = lb_ref[...]
    for d, ref in enumerate((hf0_ref, hf1_ref)):
        f = lb + (1.0 - lb) * jax.nn.sigmoid(ref[...])
        gk_ref[d] = 1.0 - f
        glf_ref[d] = jnp.log(f)


def _prep(proj, cos_t, sin_t, conv_w, avec, bvec, lb, n1, t1, t2):
    nt = proj.shape[0]
    tm = PREP_ROWS
    hpb = tm // HALO_ROWS
    n_halo = nt // HALO_ROWS
    row = lambda w, cb: pl.BlockSpec((tm, w), lambda j, cb=cb: (j, cb))
    full = lambda a: pl.BlockSpec(a.shape, lambda j: (0,) * a.ndim)
    q3 = C_DNQKV // (3 * MIX_W)
    rows_o = lambda w: pl.BlockSpec((tm, w), lambda j: (j, 0))
    dir_o = pl.BlockSpec((N_DIR, tm, MIX_W), lambda j: (0, j, 0))
    f_rows = lambda w: jax.ShapeDtypeStruct((nt, w), F32)
    f_dir = jax.ShapeDtypeStruct((N_DIR, nt, MIX_W), F32)
    return pl.pallas_call(
        functools.partial(_prep_body, n1 // tm, t1 // tm, t2 // tm),
        grid=(nt // tm,),
        in_specs=[row(3 * MIX_W, q3),
                  pl.BlockSpec((HALO_ROWS, 3 * MIX_W), lambda j: (jnp.maximum(j * hpb - 1, 0), q3)),
                  pl.BlockSpec((HALO_ROWS, 3 * MIX_W), lambda j: (jnp.minimum((j + 1) * hpb, n_halo - 1), q3)),
                  row(LANES, C_SMALL // LANES),
                  row(2 * MIX_W, C_RTQKV // (2 * MIX_W)),
                  row(MIX_W, C_HGQ // MIX_W),
                  row(MIX_W, C_HGF // MIX_W), row(MIX_W, C_HGF // MIX_W + 1),
                  rows_o(HEAD_DIM), rows_o(HEAD_DIM),
                  full(conv_w), full(avec), full(bvec), full(lb)],
        out_specs=[rows_o(MIX_W), rows_o(MIX_W), rows_o(MIX_W), rows_o(LANES), rows_o(MIX_W), rows_o(MIX_W),
                   rows_o(MIX_W), dir_o, dir_o],
        out_shape=[f_rows(MIX_W), f_rows(MIX_W), f_rows(MIX_W), f_rows(LANES), f_rows(MIX_W), f_rows(MIX_W),
                   f_rows(MIX_W), f_dir, f_dir],
        compiler_params=_params(("parallel",)),
        name="mixer_prep",
    )(proj, proj, proj, proj, proj, proj, proj, proj, cos_t, sin_t, conv_w, avec, bvec, lb)


def _mix_group(prep, proj, lg, nb, t, row0, o_prev, s_dn, s_rt, s_hg):
    dq, dk, dv, gb, rq, rk, gq, gk, glf = prep
    nt = proj.shape[0]
    prev = o_prev if o_prev is not None else (None, None, None)
    *o_dn, f_dn = _scan_call(_dn_body, "scan_delta", [(dq, "rows"), (dk, "rows"), (dv, "rows"), (gb, "rows")],
                            s_dn, nb, t, row0, nt, prev[0])
    *o_rt, f_rt = _scan_call(_rt_body, "scan_ret", [(rq, "rows"), (rk, "rows"), (proj, C_RTQKV // MIX_W + 2)],
                            s_rt, nb, t, row0, nt, prev[1], smem_inputs=(lg,))
    *o_hg, f_hg = _scan_call(_hg_body, "scan_hgrn", [(gq, "rows"), (gk, "dir"), (glf, "dir"), (proj, C_HGI // MIX_W)],
                            jnp.swapaxes(s_hg, -1, -2), nb, t, row0, nt, prev[2])
    return (o_dn, o_rt, o_hg), (f_dn, f_rt, jnp.swapaxes(f_hg, -1, -2))


def kernel(x_prompt, x_sample, state_delta, state_ret, state_hgrn, c, c_ctx, ada_w, ada_b, w_in, dn_conv_w,
           dn_a_log, dn_dt_bias, rt_log_decay, hg_lb_logits, branch_norm_w, w_branch, w_out, ln_g, ln_b,
           peer_wq, peer_subkeys, peer_u, peer_v):
    nb1, t1, d = x_prompt.shape
    nb2, t2, _ = x_sample.shape
    depth = w_in.shape[0]
    alpha = (2 * depth) ** 0.25
    n1 = nb1 * t1
    assert n1 == t2, "modulation rows are looked up per block of t2 tokens"
    rows_per_cond = t2

    p_lb = jax.nn.softmax(hg_lb_logits.astype(F32), axis=0)
    lower_bounds = jnp.cumsum(p_lb, axis=0) - p_lb[0]
    cos, sin = _grid_rotary(t2)
    cos_t = jnp.concatenate([jnp.ones((n1, HEAD_DIM), F32), jnp.tile(jnp.concatenate([cos, cos], -1), (nb2, 1))])
    sin_t = jnp.concatenate([jnp.zeros((n1, HEAD_DIM), F32), jnp.tile(jnp.concatenate([-sin, sin], -1), (nb2, 1))])
    gate_cols = slice(N_DIR * N_HEADS, 2 * N_DIR * N_HEADS)

    cond8 = jnp.zeros((8, d), F32).at[0].set(c_ctx).at[1:1 + nb2].set(c)
    mod = _ada(cond8, ada_w, ada_b)

    x = jnp.concatenate([x_prompt.reshape(n1, d), x_sample.reshape(nb2 * t2, d)], axis=0)
    zero_state = jnp.zeros((nb1, 2, N_HEADS, HEAD_DIM, HEAD_DIM), F32)
    finals = []
    for l in range(depth):
        sh1, sc1, g1, sh2, sc2, g2 = (m.reshape(8, 1, d) for m in jnp.split(mod[l], 6, axis=-1))
        wl = w_in[l]
        seg = lambda a, b: wl[:, a:b]
        small = jnp.concatenate([seg(4 * MIX_W, 4 * MIX_W + 16), jnp.zeros((d, LANES - 16), F32)], axis=1)
        o0 = 4 * MIX_W + 16
        w_cat = _bf(jnp.concatenate([seg(o0 + 9 * MIX_W, o0 + 9 * MIX_W + 3 * d),
                                     seg(0, 4 * MIX_W),
                                     seg(o0, o0 + 9 * MIX_W),
                                     small], axis=1))
        proj = _inproj(x, sc1, sh1, w_cat, rows_per_cond)
        lg = -jnp.exp(rt_log_decay[l].astype(F32))
        avec = jnp.zeros((1, LANES), F32).at[0, gate_cols].set(-jnp.exp(dn_a_log[l].astype(F32)).reshape(-1))
        bvec = jnp.zeros((1, LANES), F32).at[0, gate_cols].set(dn_dt_bias[l].astype(F32).reshape(-1))
        prep = _prep(proj, cos_t, sin_t, dn_conv_w[l], avec, bvec, lower_bounds[l].reshape(1, MIX_W), n1, t1, t2)
        o1, f1 = _mix_group(prep, proj, lg, nb1, t1, 0, None, zero_state, zero_state, zero_state)
        (o_dn, o_rt, o_hg), _ = _mix_group(prep, proj, lg, nb2, t2, n1, o1,
                                           state_delta[:, l], state_ret[:, l], state_hgrn[:, l])
        finals.append(f1)
        x = _merge(alpha, o_dn, o_rt, o_hg, proj, x, g1, branch_norm_w[l], _bf(w_branch[l]), _bf(w_out[l]),
                   ln_g[l, 0:1], ln_b[l, 0:1], rows_per_cond)
        ht, r2, e2, n_sel, e1 = _route(x, sc2, sh2, _bf(peer_wq[l].T), _bf(_interleave_subkeys(peer_subkeys[l])),
                                       rows_per_cond)
        x = _experts(alpha, ht, r2, e2, n_sel, e1, _bf(peer_u[l]), _bf(peer_v[l].T), x, g2,
                     ln_g[l, 1:2], ln_b[l, 1:2], rows_per_cond)

    y_prompt = x[:n1].reshape(nb1, t1, d)
    y_sample = x[n1:].reshape(nb2, t2, d)
    new_states = tuple(jnp.stack([f[i] for f in finals], axis=1) for i in range(3))
    return (y_prompt, y_sample) + new_states
```
